```python
import jax, jax.numpy as jnp
from jax import lax
import numpy as np

D_MODEL = 1024
BATCH = 4
SEQ = 4096
DEPTH = 4

GRID_W = 64
RET_HEADS = 4
RET_DK = 128
RET_DV = 256
NA_HEADS = 16
NA_DH = 64
NA_WIN_ROWS = 8
NA_WIN_COLS = 16
ML_HEADS = 4
ML_DK = 128
ML_DV = 256
ML_CONV_W = 5
CHUNK = 128
D_FF = -(-8 * D_MODEL // (3 * 256)) * 256
ROPE_BASE = 10000.0
EPS = 1e-6
N_BRANCH = 3
RET_QK = RET_HEADS * RET_DK
RET_V = RET_HEADS * RET_DV
NA_W = NA_HEADS * NA_DH
ML_QK = ML_HEADS * ML_DK
ML_V = ML_HEADS * ML_DV
IN_SPLITS = (RET_QK, RET_QK, RET_V, RET_V, NA_W, NA_W, NA_W, ML_QK, ML_QK, ML_V, ML_V, 4 * ML_HEADS, N_BRANCH * D_MODEL)
IN_W = sum(IN_SPLITS)

kernel_name = "hybrid_retention_natten_mlstm_encoder"


def rmsnorm(x, g):
    xf = x.astype(jnp.float32)
    y = xf * lax.rsqrt(jnp.mean(xf * xf, axis=-1, keepdims=True) + EPS)
    return (y * g.astype(jnp.float32)).astype(x.dtype)


def head_layernorm(x):
    xf = x.astype(jnp.float32)
    mu = jnp.mean(xf, axis=-1, keepdims=True)
    xc = xf - mu
    return xc * lax.rsqrt(jnp.mean(xc * xc, axis=-1, keepdims=True) + EPS)


def split_heads(t, heads):
    B, T, _ = t.shape
    return t.reshape(B, T, heads, -1).transpose(0, 2, 1, 3)


def merge_heads(t):
    B, H, T, d = t.shape
    return t.transpose(0, 2, 1, 3).reshape(B, T, H * d)


def rotary(x, cos, sin):
    x1, x2 = jnp.split(x, 2, axis=-1)
    return jnp.concatenate([x1 * cos - x2 * sin, x1 * sin + x2 * cos], axis=-1)


def retention_dir(q, k, v, log_gamma, include_diag):
    B, H, T, dk = q.shape
    dv = v.shape[-1]
    nc = T // CHUNK
    qc = q.reshape(B, H, nc, CHUNK, dk)
    kc = k.reshape(B, H, nc, CHUNK, dk)
    vc = v.reshape(B, H, nc, CHUNK, dv)
    pos = jnp.arange(CHUNK, dtype=jnp.float32)
    diff = pos[:, None] - pos[None, :]
    mask = (diff >= 0) if include_diag else (diff > 0)
    decay = jnp.where(mask, jnp.exp(log_gamma[:, None, None] * jnp.maximum(diff, 0.0)), 0.0)
    s = jnp.einsum('bhcid,bhcjd->bhcij', qc, kc) * decay[:, None]
    o_intra = jnp.einsum('bhcij,bhcje->bhcie', s, vc)
    k_w = kc * jnp.exp(log_gamma[:, None] * (CHUNK - 1 - pos))[:, None, :, None]
    s_chunk = jnp.einsum('bhcjd,bhcje->bhcde', k_w, vc)
    chunk_decay = jnp.exp(log_gamma * CHUNK)[None, :, None, None]

    def step(R, S):
        return chunk_decay * R + S, R

    _, r_prev = lax.scan(step, jnp.zeros((B, H, dk, dv), s_chunk.dtype), jnp.moveaxis(s_chunk, 2, 0))
    r_prev = jnp.moveaxis(r_prev, 0, 2)
    q_w = qc * jnp.exp(log_gamma[:, None] * (pos + 1.0))[:, None, :, None]
    o_inter = jnp.einsum('bhcid,bhcde->bhcie', q_w, r_prev)
    return (o_intra + o_inter).reshape(B, H, T, dv)


def retention_branch(rq, rk, rv, rg, decay_logit, cos, sin):
    B, T, _ = rq.shape
    q = rotary(split_heads(rq, RET_HEADS), cos, sin) * (RET_DK ** -0.5)
    k = rotary(split_heads(rk, RET_HEADS), cos, sin)
    v = split_heads(rv, RET_HEADS)
    lg_f = jax.nn.log_sigmoid(decay_logit[0].astype(jnp.float32))
    lg_b = jax.nn.log_sigmoid(decay_logit[1].astype(jnp.float32))
    fwd = retention_dir(q, k, v, lg_f, True)
    bwd = jnp.flip(retention_dir(jnp.flip(q, 2), jnp.flip(k, 2), jnp.flip(v, 2), lg_b, False), 2)
    y = merge_heads(head_layernorm(fwd + bwd)).astype(rg.dtype)
    return jax.nn.silu(rg) * y


def neighbourhood_attention(q, k, v, rpb):
    B, T, _ = q.shape
    rows = T // GRID_W
    wr = min(NA_WIN_ROWS, rows)
    wc = NA_WIN_COLS

    def grid(t):
        return t.reshape(B, rows, GRID_W, NA_HEADS, NA_DH).transpose(0, 3, 1, 2, 4)

    qg = jnp.moveaxis(grid(q), 2, 0)
    kg, vg = grid(k), grid(v)
    col = jnp.arange(GRID_W)
    col_idx = jnp.clip(col - wc // 2, 0, GRID_W - wc)[:, None] + jnp.arange(wc)[None, :]
    col_rel = col_idx - col[:, None] + (NA_WIN_COLS - 1)
    rpb_cols = rpb[:, :, col_rel]
    scale = NA_DH ** -0.5

    def row_block(args):
        r, q_r = args
        rs = jnp.clip(r - wr // 2, 0, rows - wr)
        k_r = lax.dynamic_slice_in_dim(kg, rs, wr, axis=2)[:, :, :, col_idx]
        v_r = lax.dynamic_slice_in_dim(vg, rs, wr, axis=2)[:, :, :, col_idx]
        row_rel = rs + jnp.arange(wr) - r + (NA_WIN_ROWS - 1)
        bias = rpb_cols[:, row_rel].transpose(0, 2, 1, 3)
        s = jnp.einsum('bhqd,bhrqjd->bhqrj', q_r, k_r).astype(jnp.float32) * scale + bias.astype(jnp.float32)
        p = jax.nn.softmax(s.reshape(B, NA_HEADS, GRID_W, wr * wc), axis=-1)
        p = p.reshape(B, NA_HEADS, GRID_W, wr, wc).astype(v_r.dtype)
        return jnp.einsum('bhqrj,bhrqjd->bhqd', p, v_r)

    out = lax.map(row_block, (jnp.arange(rows), qg))
    return out.transpose(1, 0, 3, 2, 4).reshape(B, T, NA_W)


def centred_depthwise_conv(x, w, b):
    K, C = w.shape
    out = lax.conv_general_dilated(x, w[:, None, :], window_strides=(1,), padding=[(K // 2, K // 2)],
                                   dimension_numbers=('NWC', 'WIO', 'NWC'), feature_group_count=C)
    return out + b


def mlstm_dir(q, k, v, i_pre, log_f):
    B, H, T, dk = q.shape
    dv = v.shape[-1]
    nc = T // CHUNK
    qc = q.reshape(B, H, nc, CHUNK, dk)
    kc = k.reshape(B, H, nc, CHUNK, dk)
    vc = v.reshape(B, H, nc, CHUNK, dv)
    ic = i_pre.reshape(B, H, nc, CHUNK)
    b = jnp.cumsum(log_f.reshape(B, H, nc, CHUNK), axis=-1)
    pos = jnp.arange(CHUNK)
    causal = pos[:, None] >= pos[None, :]
    d_log = jnp.where(causal, b[..., :, None] - b[..., None, :] + ic[..., None, :], -jnp.inf)
    b_last = b[..., -1]
    a = b_last[..., None] - b + ic
    m_loc = jnp.max(a, axis=-1)
    w = jnp.exp(a - m_loc[..., None])
    c_loc = jnp.einsum('bhcj,bhcjd,bhcje->bhcde', w, kc, vc)
    n_loc = jnp.einsum('bhcj,bhcjd->bhcd', w, kc)

    def step(carry, xs):
        C, n, m = carry
        bl, ml, cl, nl = xs
        m_new = jnp.maximum(bl + m, ml)
        s_old = jnp.exp(bl + m - m_new)
        s_new = jnp.exp(ml - m_new)
        C_new = s_old[..., None, None] * C + s_new[..., None, None] * cl
        n_new = s_old[..., None] * n + s_new[..., None] * nl
        return (C_new, n_new, m_new), (C, n, m)

    init = (jnp.zeros((B, H, dk, dv), c_loc.dtype), jnp.zeros((B, H, dk), n_loc.dtype),
            jnp.full((B, H), -jnp.inf, jnp.float32))
    xs = (jnp.moveaxis(b_last, 2, 0), jnp.moveaxis(m_loc, 2, 0), jnp.moveaxis(c_loc, 2, 0), jnp.moveaxis(n_loc, 2, 0))
    _, (c_prev, n_prev, m_prev) = lax.scan(step, init, xs)
    c_prev = jnp.moveaxis(c_prev, 0, 2)
    n_prev = jnp.moveaxis(n_prev, 0, 2)
    m_prev = jnp.moveaxis(m_prev, 0, 2)
    g = b + m_prev[..., None]
    m_t = jnp.maximum(g, jnp.max(d_log, axis=-1))
    s_inter = jnp.exp(g - m_t)
    s = jnp.einsum('bhcid,bhcjd->bhcij', qc, kc) * jnp.exp(d_log - m_t[..., None])
    num = s_inter[..., None] * jnp.einsum('bhcid,bhcde->bhcie', qc, c_prev) + jnp.einsum('bhcij,bhcje->bhcie', s, vc)
    den = s_inter * jnp.einsum('bhcid,bhcd->bhci', qc, n_prev) + jnp.sum(s, axis=-1)
    h = num / jnp.maximum(jnp.abs(den), jnp.exp(-m_t))[..., None]
    return h.reshape(B, H, T, dv)


def mlstm_branch(mq, mk, mv, mo, mgate, conv_w, conv_b, gate_b, norm_g):
    B, T, _ = mq.shape
    qk = jax.nn.silu(centred_depthwise_conv(jnp.concatenate([mq, mk], axis=-1), conv_w, conv_b))
    q_in, k_in = jnp.split(qk, 2, axis=-1)
    q = split_heads(q_in, ML_HEADS) * (ML_DK ** -0.5)
    k = split_heads(k_in, ML_HEADS)
    v = split_heads(mv, ML_HEADS)
    pre = mgate.astype(jnp.float32).reshape(B, T, 2, 2, ML_HEADS) + gate_b.astype(jnp.float32)
    pre = pre.transpose(2, 3, 0, 4, 1)
    fwd = mlstm_dir(q, k, v, pre[0, 0], jax.nn.log_sigmoid(pre[0, 1]))
    bwd = mlstm_dir(jnp.flip(q, 2), jnp.flip(k, 2), jnp.flip(v, 2),
                    jnp.flip(pre[1, 0], -1), jnp.flip(jax.nn.log_sigmoid(pre[1, 1]), -1))
    h = merge_heads(head_layernorm(fwd + jnp.flip(bwd, 2))) * norm_g.astype(jnp.float32)
    return jax.nn.sigmoid(mo) * h.astype(mo.dtype)


def setup_inputs(seed: int = 0) -> dict:
    key = jax.random.key(seed)
    ks = jax.random.split(key, 24)
    f32 = jnp.float32

    def nrm(k, shape, scale):
        return jax.random.normal(k, shape, f32) * scale

    x = nrm(ks[0], (BATCH, SEQ, D_MODEL), 1.0)
    norm1_g = 1.0 + nrm(ks[1], (DEPTH, D_MODEL), 0.02)
    w_in = nrm(ks[2], (DEPTH, D_MODEL, IN_W), D_MODEL ** -0.5)
    gate_b = nrm(ks[3], (DEPTH, N_BRANCH * D_MODEL), 0.02)
    ret_base = jnp.log(2.0 ** (5.0 + jnp.arange(RET_HEADS, dtype=f32)) - 1.0)
    ret_decay_logit = ret_base[None, None, :] + nrm(ks[4], (DEPTH, 2, RET_HEADS), 0.1)
    na_rpb = nrm(ks[5], (DEPTH, NA_HEADS, 2 * NA_WIN_ROWS - 1, 2 * NA_WIN_COLS - 1), 0.02)
    ml_conv_w = nrm(ks[6], (DEPTH, ML_CONV_W, 2 * ML_QK), ML_CONV_W ** -0.5)
    ml_conv_b = nrm(ks[7], (DEPTH, 2 * ML_QK), 0.02)
    i_bias = nrm(ks[8], (DEPTH, 2, 1, ML_HEADS), 0.1)
    f_bias = jnp.linspace(3.0, 6.0, ML_HEADS, dtype=f32)[None, None, None, :] + nrm(ks[9], (DEPTH, 2, 1, ML_HEADS), 0.1)
    ml_gate_b = jnp.concatenate([i_bias, f_bias], axis=2)
    ml_norm_g = 1.0 + nrm(ks[10], (DEPTH, ML_V), 0.02)
    w_ret_o = nrm(ks[11], (DEPTH, RET_V, D_MODEL), RET_V ** -0.5)
    w_na_o = nrm(ks[12], (DEPTH, NA_W, D_MODEL), NA_W ** -0.5)
    w_ml_o = nrm(ks[13], (DEPTH, ML_V, D_MODEL), ML_V ** -0.5)
    w_out = nrm(ks[14], (DEPTH, D_MODEL, D_MODEL), D_MODEL ** -0.5)
    norm2_g = 1.0 + nrm(ks[15], (DEPTH, D_MODEL), 0.02)
    w_ffn_up = nrm(ks[16], (DEPTH, D_MODEL, 2 * D_FF), D_MODEL ** -0.5)
    w_ffn_down = nrm(ks[17], (DEPTH, D_FF, D_MODEL), D_FF ** -0.5)
    final_g = 1.0 + nrm(ks[18], (D_MODEL,), 0.02)
    return {"x": x, "norm1_g": norm1_g, "w_in": w_in, "gate_b": gate_b, "ret_decay_logit": ret_decay_logit,
            "na_rpb": na_rpb, "ml_conv_w": ml_conv_w, "ml_conv_b": ml_conv_b, "ml_gate_b": ml_gate_b,
            "ml_norm_g": ml_norm_g, "w_ret_o": w_ret_o, "w_na_o": w_na_o, "w_ml_o": w_ml_o, "w_out": w_out,
            "norm2_g": norm2_g, "w_ffn_up": w_ffn_up, "w_ffn_down": w_ffn_down, "final_g": final_g}


def reference(x, norm1_g, w_in, gate_b, ret_decay_logit, na_rpb, ml_conv_w, ml_conv_b, ml_gate_b, ml_norm_g,
              w_ret_o, w_na_o, w_ml_o, w_out, norm2_g, w_ffn_up, w_ffn_down, final_g):
    B, T, _ = x.shape
    half = RET_DK // 2
    inv_freq = ROPE_BASE ** (-jnp.arange(half, dtype=jnp.float32) / half)
    ang = jnp.arange(T, dtype=jnp.float32)[:, None] * inv_freq[None, :]
    cos, sin = jnp.cos(ang), jnp.sin(ang)
    split_at = np.cumsum(IN_SPLITS)[:-1]
    for l in range(DEPTH):
        h = rmsnorm(x, norm1_g[l])
        proj = h @ w_in[l]
        (rq, rk, rv, rg, nq, nk, nv, mq, mk, mv, mo, mgate, mix_gate) = jnp.split(proj, split_at, axis=-1)
        y_ret = retention_branch(rq, rk, rv, rg, ret_decay_logit[l], cos, sin)
        y_na = neighbourhood_attention(nq, nk, nv, na_rpb[l])
        y_ml = mlstm_branch(mq, mk, mv, mo, mgate, ml_conv_w[l], ml_conv_b[l], ml_gate_b[l], ml_norm_g[l])
        gates = jax.nn.sigmoid((mix_gate + gate_b[l]).astype(jnp.float32)).astype(x.dtype)
        g_ret, g_na, g_ml = jnp.split(gates, N_BRANCH, axis=-1)
        merged = g_ret * (y_ret @ w_ret_o[l]) + g_na * (y_na @ w_na_o[l]) + g_ml * (y_ml @ w_ml_o[l])
        x = x + merged @ w_out[l]
        h2 = rmsnorm(x, norm2_g[l])
        a, u = jnp.split(h2 @ w_ffn_up[l], 2, axis=-1)
        x = x + (jax.nn.silu(a) * u) @ w_ffn_down[l]
    return rmsnorm(x, final_g)
```

```python
import functools

import jax
import jax.numpy as jnp
import numpy as np
from jax import lax
from jax.experimental import pallas as pl
from jax.experimental.pallas import tpu as pltpu

F32 = jnp.float32
BF16 = jnp.bfloat16

D_MODEL = 1024
GRID_W = 64
RET_HEADS, RET_DK, RET_DV = 4, 128, 256
NA_HEADS, NA_DH, NA_WIN_ROWS, NA_WIN_COLS = 16, 64, 8, 16
ML_HEADS, ML_DK, ML_DV, ML_CONV_W = 4, 128, 256, 5
CHUNK = 128
D_FF = 2816
ROPE_BASE = 10000.0
EPS = 1e-6
N_BRANCH = 3

RET_QK = RET_HEADS * RET_DK
RET_V = RET_HEADS * RET_DV
NA_W = NA_HEADS * NA_DH
ML_QK = ML_HEADS * ML_DK
ML_V = ML_HEADS * ML_DV
N_GATE = 4 * ML_HEADS
OFF_RQ = 0
OFF_RK = OFF_RQ + RET_QK
OFF_RV = OFF_RK + RET_QK
OFF_RG = OFF_RV + RET_V
OFF_NQ = OFF_RG + RET_V
OFF_NK = OFF_NQ + NA_W
OFF_NV = OFF_NK + NA_W
OFF_MQ = OFF_NV + NA_W
OFF_MK = OFF_MQ + ML_QK
OFF_MV = OFF_MK + ML_QK
OFF_MO = OFF_MV + ML_V
OFF_MIX = OFF_MO + ML_V
PROJ_W = OFF_MIX + N_BRANCH * D_MODEL
W_IN_GATE_OFF = OFF_MIX
LANES = 128
HALO = 8
FF_CHUNK = 256
VMEM_LIMIT = 56 * 1024 * 1024

NT = (((1,), (1,)), ((), ()))
TN = (((0,), (0,)), ((), ()))


def _dot(a, b):
    return jnp.dot(a, b, preferred_element_type=F32)


def _dotg(a, b, dims):
    return lax.dot_general(a, b, dims, preferred_element_type=F32)


def _sigmoid(x):
    return 1.0 / (1.0 + jnp.exp(-x))


def _cparams(sem):
    return pltpu.CompilerParams(dimension_semantics=sem, vmem_limit_bytes=VMEM_LIMIT)


def _in_proj_kernel(x_ref, g_ref, w_ref, wg_ref, o_ref, og_ref, h_ref):
    @pl.when(pl.program_id(1) == 0)
    def _():
        x = x_ref[...]
        ms = jnp.mean(x * x, axis=-1, keepdims=True)
        h = (x * lax.rsqrt(ms + EPS) * g_ref[...]).astype(BF16)
        h_ref[...] = h
        og_ref[...] = _dot(h, wg_ref[...])

    o_ref[...] = _dot(h_ref[...], w_ref[...]).astype(o_ref.dtype)


def _in_proj(x2, g, w, wg, tm=1024, tn=512):
    m = x2.shape[0]
    return pl.pallas_call(
        _in_proj_kernel,
        grid=(m // tm, PROJ_W // tn),
        in_specs=[
            pl.BlockSpec((tm, D_MODEL), lambda i, j: (i, 0)),
            pl.BlockSpec((1, D_MODEL), lambda i, j: (0, 0)),
            pl.BlockSpec((D_MODEL, tn), lambda i, j: (0, j)),
            pl.BlockSpec((D_MODEL, LANES), lambda i, j: (0, 0)),
        ],
        out_specs=[
            pl.BlockSpec((tm, tn), lambda i, j: (i, j)),
            pl.BlockSpec((tm, LANES), lambda i, j: (i, 0)),
        ],
        out_shape=[
            jax.ShapeDtypeStruct((m, PROJ_W), BF16),
            jax.ShapeDtypeStruct((m, LANES), F32),
        ],
        scratch_shapes=[pltpu.VMEM((tm, D_MODEL), BF16)],
        compiler_params=_cparams(("parallel", "arbitrary")),
        name="in_proj",
    )(x2, g, w, wg)


def _head_layernorm(o):
    mu = jnp.mean(o, axis=-1, keepdims=True)
    oc = o - mu
    return oc * lax.rsqrt(jnp.mean(oc * oc, axis=-1, keepdims=True) + EPS)


def _ret_kernel(lg_ref, q_ref, k_ref, v_ref, g_ref, cos_ref, sin_ref, o_ref,
                qr_ref, kr_ref, acc_ref, rf_ref, rb_ref, dw_ref):
    h = pl.program_id(1)
    nc = q_ref.shape[0] // CHUNK
    lgf = lg_ref[0, h]
    lgb = lg_ref[1, h]
    pi = lax.broadcasted_iota(jnp.int32, (CHUNK, CHUNK), 0).astype(F32)
    pj = lax.broadcasted_iota(jnp.int32, (CHUNK, CHUNK), 1).astype(F32)
    diff = pi - pj
    dw_ref[0] = jnp.where(diff >= 0, jnp.exp(lgf * jnp.maximum(diff, 0.0)),
                          jnp.exp(lgb * jnp.maximum(-diff, 0.0)))
    dw_ref[1] = jnp.exp(lgf * (pi + 1.0))
    dw_ref[2] = jnp.exp(lgf * (CHUNK - 1.0 - pi))
    dw_ref[3] = jnp.exp(lgb * (CHUNK - pi))
    dw_ref[4] = jnp.exp(lgb * pi)
    cdf = jnp.exp(jnp.full((1, RET_DV), lgf * CHUNK, F32))
    cdb = jnp.exp(jnp.full((1, RET_DV), lgb * CHUNK, F32))
    rf_ref[...] = jnp.zeros_like(rf_ref)
    rb_ref[...] = jnp.zeros_like(rb_ref)
    scale = RET_DK ** -0.5

    def chunk(c):
        return pl.ds(pl.multiple_of(c * CHUNK, CHUNK), CHUNK)

    def bwd_body(t, carry):
        sl = chunk(nc - 1 - t)
        q = q_ref[sl, :].astype(F32)
        k = k_ref[sl, :].astype(F32)
        cos = cos_ref[sl, :]
        sin = sin_ref[sl, :]
        qr = (q * cos + pltpu.roll(q, RET_DK // 2, 1) * sin) * scale
        kr = k * cos + pltpu.roll(k, RET_DK // 2, 1) * sin
        qr_ref[sl, :] = qr
        kr_ref[sl, :] = kr
        v = v_ref[sl, :]
        r = rb_ref[...]
        acc_ref[sl, :] = _dot((qr * dw_ref[3]).astype(BF16), r.astype(BF16))
        rb_ref[...] = r * cdb + _dotg((kr * dw_ref[4]).astype(BF16), v, TN)
        return carry

    lax.fori_loop(0, nc, bwd_body, 0)

    def fwd_body(c, carry):
        sl = chunk(c)
        qr = qr_ref[sl, :]
        kr = kr_ref[sl, :]
        v = v_ref[sl, :]
        r = rf_ref[...]
        s = _dotg(qr.astype(BF16), kr.astype(BF16), NT) * dw_ref[0]
        o = (acc_ref[sl, :] + _dot(s.astype(BF16), v)
             + _dot((qr * dw_ref[1]).astype(BF16), r.astype(BF16)))
        rf_ref[...] = r * cdf + _dotg((kr * dw_ref[2]).astype(BF16), v, TN)
        g = g_ref[sl, :].astype(F32)
        o_ref[sl, :] = (g * _sigmoid(g) * _head_layernorm(o)).astype(o_ref.dtype)
        return carry

    lax.fori_loop(0, nc, fwd_body, 0)


def _retention(proj, lg, cos2, sin2, batch, seq):
    m = proj.shape[0]
    qb, kb = OFF_RQ // RET_DK, OFF_RK // RET_DK
    vb, gb = OFF_RV // RET_DV, OFF_RG // RET_DV
    return pl.pallas_call(
        _ret_kernel,
        grid=(batch, RET_HEADS),
        in_specs=[
            pl.BlockSpec(memory_space=pltpu.SMEM),
            pl.BlockSpec((seq, RET_DK), lambda b, h: (b, qb + h)),
            pl.BlockSpec((seq, RET_DK), lambda b, h: (b, kb + h)),
            pl.BlockSpec((seq, RET_DV), lambda b, h: (b, vb + h)),
            pl.BlockSpec((seq, RET_DV), lambda b, h: (b, gb + h)),
            pl.BlockSpec((seq, RET_DK), lambda b, h: (0, 0)),
            pl.BlockSpec((seq, RET_DK), lambda b, h: (0, 0)),
        ],
        out_specs=pl.BlockSpec((seq, RET_DV), lambda b, h: (b, h)),
        out_shape=jax.ShapeDtypeStruct((m, RET_V), BF16),
        scratch_shapes=[
            pltpu.VMEM((seq, RET_DK), F32),
            pltpu.VMEM((seq, RET_DK), F32),
            pltpu.VMEM((seq, RET_DV), F32),
            pltpu.VMEM((RET_DK, RET_DV), F32),
            pltpu.VMEM((RET_DK, RET_DV), F32),
            pltpu.VMEM((5, CHUNK, CHUNK), F32),
        ],
        compiler_params=_cparams(("parallel", "arbitrary")),
        name="retention",
    )(lg, proj, proj, proj, proj, cos2, sin2)


def _na_kernel(q_ref, k_ref, v_ref, bias_ref, o_ref):
    rows = q_ref.shape[0] // GRID_W
    win = NA_WIN_ROWS * GRID_W
    scale = NA_DH ** -0.5

    def body(r, carry):
        rs = jnp.clip(r - NA_WIN_ROWS // 2, 0, rows - NA_WIN_ROWS)
        oi = r - rs
        qsl = pl.ds(pl.multiple_of(r * GRID_W, GRID_W), GRID_W)
        ksl = pl.ds(pl.multiple_of(rs * GRID_W, GRID_W), win)
        q2 = q_ref[qsl, :]
        k2 = k_ref[ksl, :]
        v2 = v_ref[ksl, :]
        outs = []
        for hh in range(LANES // NA_DH):
            lanes = slice(hh * NA_DH, (hh + 1) * NA_DH)
            q = (q2[:, lanes].astype(F32) * scale).astype(BF16)
            s = _dotg(q, k2[:, lanes], NT) + bias_ref[hh, oi]
            p = jnp.exp(s - jnp.max(s, axis=-1, keepdims=True))
            l = jnp.sum(p, axis=-1, keepdims=True)
            outs.append(_dot(p.astype(BF16), v2[:, lanes]) * (1.0 / l))
        o_ref[qsl, :] = jnp.concatenate(outs, axis=1).astype(o_ref.dtype)
        return carry

    lax.fori_loop(0, rows, body, 0)


def _na_bias_tables(rpb):
    col = np.arange(GRID_W)
    cs = np.clip(col - NA_WIN_COLS // 2, 0, GRID_W - NA_WIN_COLS)
    kc = np.arange(GRID_W)
    inwin = (kc[None, :] >= cs[:, None]) & (kc[None, :] < cs[:, None] + NA_WIN_COLS)
    col_rel = np.clip(kc[None, :] - col[:, None] + NA_WIN_COLS - 1, 0, 2 * NA_WIN_COLS - 2)
    oi = np.arange(NA_WIN_ROWS)
    krl = np.arange(NA_WIN_ROWS)
    row_rel = krl[None, :] - oi[:, None] + NA_WIN_ROWS - 1
    tab = rpb.astype(F32)[:, row_rel][:, :, :, col_rel]
    tab = jnp.where(inwin[None, None, None], tab, -jnp.inf)
    tab = tab.transpose(0, 1, 3, 2, 4)
    return tab.reshape(rpb.shape[0], NA_WIN_ROWS, GRID_W, NA_WIN_ROWS * GRID_W)


def _neighbourhood(proj, bias, batch, seq):
    m = proj.shape[0]
    hp = LANES // NA_DH
    qb, kb, vb = OFF_NQ // LANES, OFF_NK // LANES, OFF_NV // LANES
    return pl.pallas_call(
        _na_kernel,
        grid=(NA_HEADS // hp, batch),
        in_specs=[
            pl.BlockSpec((seq, LANES), lambda p, b: (b, qb + p)),
            pl.BlockSpec((seq, LANES), lambda p, b: (b, kb + p)),
            pl.BlockSpec((seq, LANES), lambda p, b: (b, vb + p)),
            pl.BlockSpec((hp, NA_WIN_ROWS, GRID_W, NA_WIN_ROWS * GRID_W), lambda p, b: (p, 0, 0, 0)),
        ],
        out_specs=pl.BlockSpec((seq, LANES), lambda p, b: (b, p)),
        out_shape=jax.ShapeDtypeStruct((m, NA_W), BF16),
        compiler_params=_cparams(("parallel", "arbitrary")),
        name="neighbourhood",
    )(proj, proj, proj, bias)


def _log_sigmoid(x):
    return jnp.minimum(x, 0.0) - jnp.log1p(jnp.exp(-jnp.abs(x)))


def _split3(x):
    hi = x.astype(BF16)
    r1 = x - hi.astype(F32)
    mid = r1.astype(BF16)
    lo = (r1 - mid.astype(F32)).astype(BF16)
    return hi, mid, lo


def _gate_kernel(pre_ref, b_ref, col_ref, row_ref):
    n_sub = pre_ref.shape[0] // CHUNK
    pi = lax.broadcasted_iota(jnp.int32, (CHUNK, CHUNK), 0)
    pj = lax.broadcasted_iota(jnp.int32, (CHUNK, CHUNK), 1)
    tri = jnp.where(pj <= pi, 1.0, 0.0).astype(BF16)
    lane = lax.broadcasted_iota(jnp.int32, (CHUNK, LANES), 1)
    kind = (lane // ML_HEADS) % 2
    is_fwd_f = (lane < N_GATE) & (kind == 1) & (lane < 2 * ML_HEADS)
    is_bwd_f = (lane < N_GATE) & (kind == 1) & (lane >= 2 * ML_HEADS)
    for s in range(n_sub):
        sl = slice(s * CHUNK, (s + 1) * CHUNK)
        x = pre_ref[sl, :] + b_ref[...]
        ls = _log_sigmoid(x)
        hi, mid, lo = _split3(ls)
        cum = _dot(tri, hi) + _dot(tri, mid) + _dot(tri, lo)
        rev = cum[CHUNK - 1:CHUNK, :] - cum + ls
        out = jnp.where(is_fwd_f, cum, jnp.where(is_bwd_f, rev, x))
        col_ref[sl, :] = out
        row_ref[0, :, sl] = out.T[:N_GATE, :]


def _gates(pre, bias, batch, seq, sub=8):
    m = pre.shape[0]
    rows = sub * CHUNK
    per_b = seq // rows
    return pl.pallas_call(
        _gate_kernel,
        grid=(batch, per_b),
        in_specs=[
            pl.BlockSpec((rows, LANES), lambda b, i: (b * per_b + i, 0)),
            pl.BlockSpec((1, LANES), lambda b, i: (0, 0)),
        ],
        out_specs=[
            pl.BlockSpec((rows, LANES), lambda b, i: (b * per_b + i, 0)),
            pl.BlockSpec((1, N_GATE, rows), lambda b, i: (b, 0, i)),
        ],
        out_shape=[
            jax.ShapeDtypeStruct((m, LANES), F32),
            jax.ShapeDtypeStruct((batch, N_GATE, seq), F32),
        ],
        compiler_params=_cparams(("parallel", "arbitrary")),
        name="mlstm_gates",
    )(pre, bias)


def _ml_direction(q, k, v, ic, bc, ir, br, b_last, mask, c_ref, n_ref, m_ref):
    c_prev = c_ref[...]
    n_prev = n_ref[...]
    m_prev = m_ref[...]
    d = jnp.where(mask, bc - br + ir, -jnp.inf)
    g = bc + m_prev
    m_t = jnp.maximum(g, jnp.max(d, axis=1, keepdims=True))
    s_inter = jnp.exp(g - m_t)
    qb = q.astype(BF16)
    s = _dotg(qb, k.astype(BF16), NT) * jnp.exp(d - m_t)
    num = s_inter * _dot(qb, c_prev.astype(BF16)) + _dot(s.astype(BF16), v)
    den = (s_inter * jnp.sum(q * n_prev, axis=1, keepdims=True)
           + jnp.sum(s, axis=1, keepdims=True))
    out = num * (1.0 / jnp.maximum(jnp.abs(den), jnp.exp(-m_t)))
    a = b_last - bc + ic
    m_loc = jnp.max(a, axis=0, keepdims=True)
    kw = k * jnp.exp(a - m_loc)
    m_new = jnp.maximum(b_last + m_prev, m_loc)
    s_old = jnp.exp(b_last + m_prev - m_new)
    s_new = jnp.exp(m_loc - m_new)
    c_ref[...] = s_old * c_prev + s_new * _dotg(kw.astype(BF16), v, TN)
    n_ref[...] = s_old * n_prev + s_new * jnp.sum(kw, axis=0, keepdims=True)
    m_ref[...] = m_new
    return out


def _ml_kernel(q_ref, k_ref, v_ref, og_ref, gcol_ref, grow_ref, cwq_ref, cwk_ref, cbq_ref, cbk_ref,
               ng_ref, o_ref, xq_ref, xk_ref, qc_ref, kc_ref, acc_ref, c_ref, n_ref, m_ref):
    h = pl.program_id(1)
    seq = q_ref.shape[0]
    nc = seq // CHUNK
    scale = ML_DK ** -0.5

    def chunk(c):
        return pl.ds(pl.multiple_of(c * CHUNK, CHUNK), CHUNK)

    zero_halo = jnp.zeros((HALO, ML_DK), F32)
    for ref in (xq_ref, xk_ref):
        ref[0:HALO, :] = zero_halo
        ref[seq + HALO:seq + 2 * HALO, :] = zero_halo

    def stage_body(c, carry):
        dst = pl.ds(pl.multiple_of(c * CHUNK + HALO, HALO), CHUNK)
        xq_ref[dst, :] = q_ref[chunk(c), :].astype(F32)
        xk_ref[dst, :] = k_ref[chunk(c), :].astype(F32)
        return carry

    lax.fori_loop(0, nc, stage_body, 0)

    def conv_body(c, carry):
        wsl = pl.ds(pl.multiple_of(c * CHUNK, CHUNK), CHUNK + 2 * HALO)
        for x_ref, w_ref, b_ref, dst_ref, mul in ((xq_ref, cwq_ref, cbq_ref, qc_ref, scale),
                                                  (xk_ref, cwk_ref, cbk_ref, kc_ref, 1.0)):
            win = x_ref[wsl, :]
            y = b_ref[...] + jnp.zeros((CHUNK, ML_DK), F32)
            for j in range(ML_CONV_W):
                off = HALO - ML_CONV_W // 2 + j
                y = y + win[off:off + CHUNK, :] * w_ref[j:j + 1, :]
            y = y * _sigmoid(y)
            dst_ref[chunk(c), :] = y * mul if mul != 1.0 else y
        return carry

    lax.fori_loop(0, nc, conv_body, 0)

    lane = lax.broadcasted_iota(jnp.int32, (CHUNK, LANES), 1)
    sub = lax.broadcasted_iota(jnp.int32, (N_GATE, CHUNK), 0)
    pi = lax.broadcasted_iota(jnp.int32, (CHUNK, CHUNK), 0)
    pj = lax.broadcasted_iota(jnp.int32, (CHUNK, CHUNK), 1)

    def gate_vectors(c, i_idx, f_idx):
        gc = gcol_ref[chunk(c), :]
        gr = grow_ref[0, :, chunk(c)]
        ic = jnp.sum(jnp.where(lane == i_idx, gc, 0.0), axis=1, keepdims=True)
        bc = jnp.sum(jnp.where(lane == f_idx, gc, 0.0), axis=1, keepdims=True)
        ir = jnp.sum(jnp.where(sub == i_idx, gr, 0.0), axis=0, keepdims=True)
        br = jnp.sum(jnp.where(sub == f_idx, gr, 0.0), axis=0, keepdims=True)
        return ic, bc, ir, br

    def reset_state():
        c_ref[...] = jnp.zeros_like(c_ref)
        n_ref[...] = jnp.zeros_like(n_ref)
        m_ref[...] = jnp.full(m_ref.shape, -jnp.inf, F32)

    reset_state()

    def bwd_body(t, carry):
        c = nc - 1 - t
        sl = chunk(c)
        ic, bc, ir, br = gate_vectors(c, 2 * ML_HEADS + h, 3 * ML_HEADS + h)
        acc_ref[sl, :] = _ml_direction(qc_ref[sl, :], kc_ref[sl, :], v_ref[sl, :], ic, bc, ir, br,
                                       bc[0:1, :], pj >= pi, c_ref, n_ref, m_ref)
        return carry

    lax.fori_loop(0, nc, bwd_body, 0)
    reset_state()

    def fwd_body(c, carry):
        sl = chunk(c)
        ic, bc, ir, br = gate_vectors(c, h, ML_HEADS + h)
        out = _ml_direction(qc_ref[sl, :], kc_ref[sl, :], v_ref[sl, :], ic, bc, ir, br,
                            bc[CHUNK - 1:CHUNK, :], pj <= pi, c_ref, n_ref, m_ref)
        y = _head_layernorm(out + acc_ref[sl, :]) * ng_ref[...]
        o_ref[sl, :] = (_sigmoid(og_ref[sl, :].astype(F32)) * y).astype(o_ref.dtype)
        return carry

    lax.fori_loop(0, nc, fwd_body, 0)


def _mlstm(proj, gcol, grow, conv_w, conv_b, norm_g, batch, seq):
    m = proj.shape[0]
    qb, kb = OFF_MQ // ML_DK, OFF_MK // ML_DK
    vb, ob = OFF_MV // ML_DV, OFF_MO // ML_DV
    kw = conv_w.shape[0]
    return pl.pallas_call(
        _ml_kernel,
        grid=(batch, ML_HEADS),
        in_specs=[
            pl.BlockSpec((seq, ML_DK), lambda b, h: (b, qb + h)),
            pl.BlockSpec((seq, ML_DK), lambda b, h: (b, kb + h)),
            pl.BlockSpec((seq, ML_DV), lambda b, h: (b, vb + h)),
            pl.BlockSpec((seq, ML_DV), lambda b, h: (b, ob + h)),
            pl.BlockSpec((seq, LANES), lambda b, h: (b, 0)),
            pl.BlockSpec((1, N_GATE, seq), lambda b, h: (b, 0, 0)),
            pl.BlockSpec((kw, ML_DK), lambda b, h: (0, h)),
            pl.BlockSpec((kw, ML_DK), lambda b, h: (0, ML_HEADS + h)),
            pl.BlockSpec((1, ML_DK), lambda b, h: (0, h)),
            pl.BlockSpec((1, ML_DK), lambda b, h: (0, ML_HEADS + h)),
            pl.BlockSpec((1, ML_DV), lambda b, h: (0, h)),
        ],
        out_specs=pl.BlockSpec((seq, ML_DV), lambda b, h: (b, h)),
        out_shape=jax.ShapeDtypeStruct((m, ML_V), BF16),
        scratch_shapes=[
            pltpu.VMEM((seq + 2 * HALO, ML_DK), F32),
            pltpu.VMEM((seq + 2 * HALO, ML_DK), F32),
            pltpu.VMEM((seq, ML_DK), F32),
            pltpu.VMEM((seq, ML_DK), F32),
            pltpu.VMEM((seq, ML_DV), F32),
            pltpu.VMEM((ML_DK, ML_DV), F32),
            pltpu.VMEM((1, ML_DK), F32),
            pltpu.VMEM((1, 1), F32),
        ],
        compiler_params=_cparams(("parallel", "arbitrary")),
        name="mlstm",
    )(proj, proj, proj, proj, gcol, grow, conv_w, conv_w, conv_b, conv_b, norm_g)


def _merge_kernel(x_ref, yr_ref, yn_ref, ym_ref, mix_ref, gb_ref, wr_ref, wn_ref, wm_ref, wo_ref,
                  o_ref):
    merged = None
    for i, (y_ref, w_ref) in enumerate(((yr_ref, wr_ref), (yn_ref, wn_ref), (ym_ref, wm_ref))):
        cols = slice(i * D_MODEL, (i + 1) * D_MODEL)
        gate = _sigmoid(mix_ref[:, cols].astype(F32) + gb_ref[:, cols])
        term = gate * _dot(y_ref[...], w_ref[...])
        merged = term if merged is None else merged + term
    o_ref[...] = x_ref[...] + _dot(merged.astype(BF16), wo_ref[...])


def _merge(x2, y_ret, y_na, y_ml, proj, gate_b, w_ret_o, w_na_o, w_ml_o, w_out, tm=512):
    m = x2.shape[0]
    row = lambda i: (i, 0)
    const = lambda i: (0, 0)
    wspec = pl.BlockSpec((D_MODEL, D_MODEL), const)
    return pl.pallas_call(
        _merge_kernel,
        grid=(m // tm,),
        in_specs=[
            pl.BlockSpec((tm, D_MODEL), row),
            pl.BlockSpec((tm, RET_V), row),
            pl.BlockSpec((tm, NA_W), row),
            pl.BlockSpec((tm, ML_V), row),
            pl.BlockSpec((tm, N_BRANCH * D_MODEL), lambda i: (i, OFF_MIX // (N_BRANCH * D_MODEL))),
            pl.BlockSpec((1, N_BRANCH * D_MODEL), const),
            wspec, wspec, wspec, wspec,
        ],
        out_specs=pl.BlockSpec((tm, D_MODEL), row),
        out_shape=jax.ShapeDtypeStruct((m, D_MODEL), F32),
        compiler_params=_cparams(("parallel",)),
        name="merge_out",
    )(x2, y_ret, y_na, y_ml, proj, gate_b, w_ret_o, w_na_o, w_ml_o, w_out)


def _rms(x, g):
    return x * lax.rsqrt(jnp.mean(x * x, axis=-1, keepdims=True) + EPS) * g


def _ffn_kernel(x_ref, g_ref, wu_ref, wd_ref, fg_ref, o_ref, *, final_norm):
    x = x_ref[...]
    h = _rms(x, g_ref[...]).astype(BF16)
    acc = x
    for c in range(D_FF // FF_CHUNK):
        cols = slice(c * FF_CHUNK, (c + 1) * FF_CHUNK)
        a = _dot(h, wu_ref[:, cols])
        u = _dot(h, wu_ref[:, D_FF + c * FF_CHUNK:D_FF + (c + 1) * FF_CHUNK])
        act = (a * _sigmoid(a) * u).astype(BF16)
        acc = acc + _dot(act, wd_ref[cols, :])
    o_ref[...] = _rms(acc, fg_ref[...]) if final_norm else acc


def _ffn(x2, g, w_up, w_down, final_g, final_norm, tm=512):
    m = x2.shape[0]
    const = lambda i: (0, 0)
    return pl.pallas_call(
        functools.partial(_ffn_kernel, final_norm=final_norm),
        grid=(m // tm,),
        in_specs=[
            pl.BlockSpec((tm, D_MODEL), lambda i: (i, 0)),
            pl.BlockSpec((1, D_MODEL), const),
            pl.BlockSpec((D_MODEL, 2 * D_FF), const),
            pl.BlockSpec((D_FF, D_MODEL), const),
            pl.BlockSpec((1, D_MODEL), const),
        ],
        out_specs=pl.BlockSpec((tm, D_MODEL), lambda i: (i, 0)),
        out_shape=jax.ShapeDtypeStruct((m, D_MODEL), F32),
        compiler_params=_cparams(("parallel",)),
        name="ffn",
    )(x2, g, w_up, w_down, final_g)


def kernel(x, norm1_g, w_in, gate_b, ret_decay_logit, na_rpb, ml_conv_w, ml_conv_b, ml_gate_b, ml_norm_g,
           w_ret_o, w_na_o, w_ml_o, w_out, norm2_g, w_ffn_up, w_ffn_down, final_g):
    batch, seq, _ = x.shape
    depth = w_in.shape[0]
    assert seq % (8 * CHUNK) == 0 and seq // GRID_W >= NA_WIN_ROWS

    half = RET_DK // 2
    inv_freq = ROPE_BASE ** (-jnp.arange(half, dtype=F32) / half)
    ang = jnp.arange(seq, dtype=F32)[:, None] * inv_freq[None, :]
    cos, sin = jnp.cos(ang), jnp.sin(ang)
    cos2 = jnp.concatenate([cos, cos], axis=-1)
    sin2 = jnp.concatenate([-sin, sin], axis=-1)

    g0, g1 = W_IN_GATE_OFF, W_IN_GATE_OFF + N_GATE
    w_main = jnp.concatenate([w_in[:, :, :g0], w_in[:, :, g1:]], axis=-1).astype(BF16)
    w_gate = jnp.pad(w_in[:, :, g0:g1], ((0, 0), (0, 0), (0, LANES - N_GATE))).astype(BF16)
    ml_bias = jnp.pad(ml_gate_b.reshape(depth, 1, N_GATE).astype(F32), ((0, 0), (0, 0), (0, LANES - N_GATE)))
    conv_w = jnp.pad(ml_conv_w.astype(F32), ((0, 0), (0, 8 - ML_CONV_W), (0, 0)))
    lg = jax.nn.log_sigmoid(ret_decay_logit.astype(F32))
    w_ret_o, w_na_o, w_ml_o, w_out = (w.astype(BF16) for w in (w_ret_o, w_na_o, w_ml_o, w_out))
    w_ffn_up, w_ffn_down = w_ffn_up.astype(BF16), w_ffn_down.astype(BF16)

    x2 = x.reshape(batch * seq, D_MODEL).astype(F32)
    for l in range(depth):
        proj, pre = _in_proj(x2, norm1_g[l][None], w_main[l], w_gate[l])
        y_ret = _retention(proj, lg[l], cos2, sin2, batch, seq)
        y_na = _neighbourhood(proj, _na_bias_tables(na_rpb[l]), batch, seq)
        gcol, grow = _gates(pre, ml_bias[l], batch, seq)
        y_ml = _mlstm(proj, gcol, grow, conv_w[l], ml_conv_b[l][None].astype(F32),
                      ml_norm_g[l][None].astype(F32), batch, seq)
        x2 = _merge(x2, y_ret, y_na, y_ml, proj, gate_b[l][None].astype(F32),
                    w_ret_o[l], w_na_o[l], w_ml_o[l], w_out[l])
        x2 = _ffn(x2, norm2_g[l][None].astype(F32), w_ffn_up[l], w_ffn_down[l],
                  final_g[None].astype(F32), final_norm=(l == depth - 1))
    return x2.reshape(batch, seq, D_MODEL).astype(x.dtype)
```

```python
import functools

import jax
import jax.numpy as jnp
import numpy as np
from jax import lax
from jax.experimental import pallas as pl
from jax.experimental.pallas import tpu as pltpu

F32 = jnp.float32
BF16 = jnp.bfloat16

D_MODEL = 1024
GRID_W = 64
RET_HEADS, RET_DK, RET_DV = 4, 128, 256
NA_HEADS, NA_DH, NA_WIN_ROWS, NA_WIN_COLS = 16, 64, 8, 16
ML_HEADS, ML_DK, ML_DV, ML_CONV_W = 4, 128, 256, 5
CHUNK = 128
D_FF = 2816
ROPE_BASE = 10000.0
EPS = 1e-6
N_BRANCH = 3

RET_QK = RET_HEADS * RET_DK
RET_V = RET_HEADS * RET_DV
NA_W = NA_HEADS * NA_DH
ML_QK = ML_HEADS * ML_DK
ML_V = ML_HEADS * ML_DV
N_GATE = 4 * ML_HEADS
OFF_RQ = 0
OFF_RK = OFF_RQ + RET_QK
OFF_RV = OFF_RK + RET_QK
OFF_RG = OFF_RV + RET_V
OFF_NQ = OFF_RG + RET_V
OFF_NK = OFF_NQ + NA_W
OFF_NV = OFF_NK + NA_W
OFF_MQ = OFF_NV + NA_W
OFF_MK = OFF_MQ + ML_QK
OFF_MV = OFF_MK + ML_QK
OFF_MO = OFF_MV + ML_V
OFF_MIX = OFF_MO + ML_V
PROJ_W = OFF_MIX + N_BRANCH * D_MODEL
W_IN_GATE_OFF = OFF_MIX
LANES = 128
HALO = 8
FF_CHUNK = 256
RET_UNROLL = 2
VMEM_LIMIT = 56 * 1024 * 1024

NT = (((1,), (1,)), ((), ()))
TN = (((0,), (0,)), ((), ()))


def _dot(a, b):
    return jnp.dot(a, b, preferred_element_type=F32)


def _dotg(a, b, dims):
    return lax.dot_general(a, b, dims, preferred_element_type=F32)


def _sigmoid(x):
    return 1.0 / (1.0 + jnp.exp(-x))


def _cparams(sem):
    return pltpu.CompilerParams(dimension_semantics=sem, vmem_limit_bytes=VMEM_LIMIT)


def _in_proj_kernel(x_ref, g_ref, w_ref, wg_ref, o_ref, og_ref, h_ref):
    @pl.when(pl.program_id(1) == 0)
    def _():
        x = x_ref[...]
        ms = jnp.mean(x * x, axis=-1, keepdims=True)
        h = (x * lax.rsqrt(ms + EPS) * g_ref[...]).astype(BF16)
        h_ref[...] = h
        og_ref[...] = _dot(h, wg_ref[...])

    o_ref[...] = _dot(h_ref[...], w_ref[...]).astype(o_ref.dtype)


def _in_proj(x2, g, w, wg, tm=1024, tn=2048):
    m = x2.shape[0]
    return pl.pallas_call(
        _in_proj_kernel,
        grid=(m // tm, PROJ_W // tn),
        in_specs=[
            pl.BlockSpec((tm, D_MODEL), lambda i, j: (i, 0)),
            pl.BlockSpec((1, D_MODEL), lambda i, j: (0, 0)),
            pl.BlockSpec((D_MODEL, tn), lambda i, j: (0, j)),
            pl.BlockSpec((D_MODEL, LANES), lambda i, j: (0, 0)),
        ],
        out_specs=[
            pl.BlockSpec((tm, tn), lambda i, j: (i, j)),
            pl.BlockSpec((tm, LANES), lambda i, j: (i, 0)),
        ],
        out_shape=[
            jax.ShapeDtypeStruct((m, PROJ_W), BF16),
            jax.ShapeDtypeStruct((m, LANES), F32),
        ],
        scratch_shapes=[pltpu.VMEM((tm, D_MODEL), BF16)],
        compiler_params=_cparams(("parallel", "arbitrary")),
        name="in_proj",
    )(x2, g, w, wg)


def _head_layernorm(o):
    mu = jnp.mean(o, axis=-1, keepdims=True)
    oc = o - mu
    return oc * lax.rsqrt(jnp.mean(oc * oc, axis=-1, keepdims=True) + EPS)


def _ret_kernel(lg_ref, q_ref, k_ref, v_ref, g_ref, cos_ref, sin_ref, o_ref,
                kr_ref, s_ref, r_ref, rf_ref, rb_ref, dw_ref):
    h = pl.program_id(1)
    nc = q_ref.shape[0] // CHUNK
    lgf = lg_ref[0, h]
    lgb = lg_ref[1, h]
    pi = lax.broadcasted_iota(jnp.int32, (CHUNK, CHUNK), 0).astype(F32)
    pj = lax.broadcasted_iota(jnp.int32, (CHUNK, CHUNK), 1).astype(F32)
    diff = pi - pj
    dw_ref[0] = jnp.where(diff >= 0, jnp.exp(lgf * jnp.maximum(diff, 0.0)),
                          jnp.exp(lgb * jnp.maximum(-diff, 0.0)))
    dw_ref[1] = jnp.exp(lgf * (pi + 1.0))
    dw_ref[2] = jnp.exp(lgf * (CHUNK - 1.0 - pi))
    dw_ref[3] = jnp.exp(lgb * (CHUNK - pi))
    dw_ref[4] = jnp.exp(lgb * pi)
    cdf = jnp.exp(jnp.full((1, RET_DV), lgf * CHUNK, F32))
    cdb = jnp.exp(jnp.full((1, RET_DV), lgb * CHUNK, F32))
    scale = RET_DK ** -0.5

    def chunk(c):
        return pl.ds(pl.multiple_of(c * CHUNK, CHUNK), CHUNK)

    def rotary(x_ref, sl):
        x = x_ref[sl, :].astype(F32)
        return x * cos_ref[sl, :] + pltpu.roll(x, RET_DK // 2, 1) * sin_ref[sl, :]

    def state_body(t, carry):
        for u in range(RET_UNROLL):
            c = t * RET_UNROLL + u
            sl = chunk(c)
            kr = rotary(k_ref, sl)
            kr_ref[sl, :] = kr.astype(BF16)
            kcat = jnp.concatenate([(kr * dw_ref[2]).astype(BF16), (kr * dw_ref[4]).astype(BF16)], axis=1)
            s_ref[c] = _dotg(kcat, v_ref[sl, :], TN)
        return carry

    lax.fori_loop(0, nc // RET_UNROLL, state_body, 0)

    rf_ref[...] = jnp.zeros_like(rf_ref)
    rb_ref[...] = jnp.zeros_like(rb_ref)

    def scan_body(t, carry):
        cb = nc - 1 - t
        rf = rf_ref[...]
        rb = rb_ref[...]
        r_ref[t, 0:RET_DK, :] = rf.astype(BF16)
        r_ref[cb, RET_DK:2 * RET_DK, :] = rb.astype(BF16)
        rf_ref[...] = rf * cdf + s_ref[t, 0:RET_DK, :]
        rb_ref[...] = rb * cdb + s_ref[cb, RET_DK:2 * RET_DK, :]
        return carry

    lax.fori_loop(0, nc, scan_body, 0)

    def out_body(t, carry):
        for u in range(RET_UNROLL):
            c = t * RET_UNROLL + u
            sl = chunk(c)
            qr = rotary(q_ref, sl) * scale
            v = v_ref[sl, :]
            s = _dotg(qr.astype(BF16), kr_ref[sl, :], NT) * dw_ref[0]
            qcat = jnp.concatenate([(qr * dw_ref[1]).astype(BF16), (qr * dw_ref[3]).astype(BF16)], axis=1)
            o = _dot(s.astype(BF16), v) + _dot(qcat, r_ref[c])
            g = g_ref[sl, :].astype(F32)
            o_ref[sl, :] = (g * _sigmoid(g) * _head_layernorm(o)).astype(o_ref.dtype)
        return carry

    lax.fori_loop(0, nc // RET_UNROLL, out_body, 0)


def _retention(proj, lg, cos2, sin2, batch, seq):
    m = proj.shape[0]
    nc = seq // CHUNK
    qb, kb = OFF_RQ // RET_DK, OFF_RK // RET_DK
    vb, gb = OFF_RV // RET_DV, OFF_RG // RET_DV
    return pl.pallas_call(
        _ret_kernel,
        grid=(batch, RET_HEADS),
        in_specs=[
            pl.BlockSpec(memory_space=pltpu.SMEM),
            pl.BlockSpec((seq, RET_DK), lambda b, h: (b, qb + h)),
            pl.BlockSpec((seq, RET_DK), lambda b, h: (b, kb + h)),
            pl.BlockSpec((seq, RET_DV), lambda b, h: (b, vb + h)),
            pl.BlockSpec((seq, RET_DV), lambda b, h: (b, gb + h)),
            pl.BlockSpec((seq, RET_DK), lambda b, h: (0, 0)),
            pl.BlockSpec((seq, RET_DK), lambda b, h: (0, 0)),
        ],
        out_specs=pl.BlockSpec((seq, RET_DV), lambda b, h: (b, h)),
        out_shape=jax.ShapeDtypeStruct((m, RET_V), BF16),
        scratch_shapes=[
            pltpu.VMEM((seq, RET_DK), BF16),
            pltpu.VMEM((nc, 2 * RET_DK, RET_DV), F32),
            pltpu.VMEM((nc, 2 * RET_DK, RET_DV), BF16),
            pltpu.VMEM((RET_DK, RET_DV), F32),
            pltpu.VMEM((RET_DK, RET_DV), F32),
            pltpu.VMEM((5, CHUNK, CHUNK), F32),
        ],
        compiler_params=_cparams(("parallel", "arbitrary")),
        name="retention",
    )(lg, proj, proj, proj, proj, cos2, sin2)


def _na_kernel(q_ref, k_ref, v_ref, bias_ref, o_ref):
    rows = q_ref.shape[0] // GRID_W
    win = NA_WIN_ROWS * GRID_W
    scale = NA_DH ** -0.5

    def body(r, carry):
        rs = jnp.clip(r - NA_WIN_ROWS // 2, 0, rows - NA_WIN_ROWS)
        oi = r - rs
        qsl = pl.ds(pl.multiple_of(r * GRID_W, GRID_W), GRID_W)
        ksl = pl.ds(pl.multiple_of(rs * GRID_W, GRID_W), win)
        q2 = q_ref[qsl, :]
        k2 = k_ref[ksl, :]
        v2 = v_ref[ksl, :]
        outs = []
        for hh in range(LANES // NA_DH):
            lanes = slice(hh * NA_DH, (hh + 1) * NA_DH)
            q = (q2[:, lanes].astype(F32) * scale).astype(BF16)
            s = _dotg(q, k2[:, lanes], NT) + bias_ref[hh, oi]
            p = jnp.exp(s - jnp.max(s, axis=-1, keepdims=True))
            l = jnp.sum(p, axis=-1, keepdims=True)
            outs.append(_dot(p.astype(BF16), v2[:, lanes]) * (1.0 / l))
        o_ref[qsl, :] = jnp.concatenate(outs, axis=1).astype(o_ref.dtype)
        return carry

    lax.fori_loop(0, rows, body, 0)


def _na_bias_tables(rpb):
    col = np.arange(GRID_W)
    cs = np.clip(col - NA_WIN_COLS // 2, 0, GRID_W - NA_WIN_COLS)
    kc = np.arange(GRID_W)
    inwin = (kc[None, :] >= cs[:, None]) & (kc[None, :] < cs[:, None] + NA_WIN_COLS)
    col_rel = np.clip(kc[None, :] - col[:, None] + NA_WIN_COLS - 1, 0, 2 * NA_WIN_COLS - 2)
    oi = np.arange(NA_WIN_ROWS)
    krl = np.arange(NA_WIN_ROWS)
    row_rel = krl[None, :] - oi[:, None] + NA_WIN_ROWS - 1
    tab = rpb.astype(F32)[:, row_rel][:, :, :, col_rel]
    tab = jnp.where(inwin[None, None, None], tab, -jnp.inf)
    tab = tab.transpose(0, 1, 3, 2, 4)
    return tab.reshape(rpb.shape[0], NA_WIN_ROWS, GRID_W, NA_WIN_ROWS * GRID_W)


def _neighbourhood(proj, bias, batch, seq):
    m = proj.shape[0]
    hp = LANES // NA_DH
    qb, kb, vb = OFF_NQ // LANES, OFF_NK // LANES, OFF_NV // LANES
    return pl.pallas_call(
        _na_kernel,
        grid=(NA_HEADS // hp, batch),
        in_specs=[
            pl.BlockSpec((seq, LANES), lambda p, b: (b, qb + p)),
            pl.BlockSpec((seq, LANES), lambda p, b: (b, kb + p)),
            pl.BlockSpec((seq, LANES), lambda p, b: (b, vb + p)),
            pl.BlockSpec((hp, NA_WIN_ROWS, GRID_W, NA_WIN_ROWS * GRID_W), lambda p, b: (p, 0, 0, 0)),
        ],
        out_specs=pl.BlockSpec((seq, LANES), lambda p, b: (b, p)),
        out_shape=jax.ShapeDtypeStruct((m, NA_W), BF16),
        compiler_params=_cparams(("parallel", "arbitrary")),
        name="neighbourhood",
    )(proj, proj, proj, bias)


def _log_sigmoid(x):
    return jnp.minimum(x, 0.0) - jnp.log1p(jnp.exp(-jnp.abs(x)))


def _split3(x):
    hi = x.astype(BF16)
    r1 = x - hi.astype(F32)
    mid = r1.astype(BF16)
    lo = (r1 - mid.astype(F32)).astype(BF16)
    return hi, mid, lo


def _gate_kernel(pre_ref, b_ref, col_ref, row_ref):
    n_sub = pre_ref.shape[0] // CHUNK
    pi = lax.broadcasted_iota(jnp.int32, (CHUNK, CHUNK), 0)
    pj = lax.broadcasted_iota(jnp.int32, (CHUNK, CHUNK), 1)
    tri = jnp.where(pj <= pi, 1.0, 0.0).astype(BF16)
    lane = lax.broadcasted_iota(jnp.int32, (CHUNK, LANES), 1)
    kind = (lane // ML_HEADS) % 2
    is_fwd_f = (lane < N_GATE) & (kind == 1) & (lane < 2 * ML_HEADS)
    is_bwd_f = (lane < N_GATE) & (kind == 1) & (lane >= 2 * ML_HEADS)
    for s in range(n_sub):
        sl = slice(s * CHUNK, (s + 1) * CHUNK)
        x = pre_ref[sl, :] + b_ref[...]
        ls = _log_sigmoid(x)
        hi, mid, lo = _split3(ls)
        cum = _dot(tri, hi) + _dot(tri, mid) + _dot(tri, lo)
        rev = cum[CHUNK - 1:CHUNK, :] - cum + ls
        out = jnp.where(is_fwd_f, cum, jnp.where(is_bwd_f, rev, x))
        col_ref[sl, :] = out
        row_ref[0, :, sl] = out.T[:N_GATE, :]


def _gates(pre, bias, batch, seq, sub=8):
    m = pre.shape[0]
    rows = sub * CHUNK
    per_b = seq // rows
    return pl.pallas_call(
        _gate_kernel,
        grid=(batch, per_b),
        in_specs=[
            pl.BlockSpec((rows, LANES), lambda b, i: (b * per_b + i, 0)),
            pl.BlockSpec((1, LANES), lambda b, i: (0, 0)),
        ],
        out_specs=[
            pl.BlockSpec((rows, LANES), lambda b, i: (b * per_b + i, 0)),
            pl.BlockSpec((1, N_GATE, rows), lambda b, i: (b, 0, i)),
        ],
        out_shape=[
            jax.ShapeDtypeStruct((m, LANES), F32),
            jax.ShapeDtypeStruct((batch, N_GATE, seq), F32),
        ],
        compiler_params=_cparams(("parallel", "arbitrary")),
        name="mlstm_gates",
    )(pre, bias)


def _ml_direction(q, k, v, ic, bc, ir, br, b_last, mask, c_ref, n_ref, m_ref):
    c_prev = c_ref[...]
    n_prev = n_ref[...]
    m_prev = m_ref[...]
    d = jnp.where(mask, bc - br + ir, -jnp.inf)
    g = bc + m_prev
    m_t = jnp.maximum(g, jnp.max(d, axis=1, keepdims=True))
    s_inter = jnp.exp(g - m_t)
    qb = q.astype(BF16)
    s = _dotg(qb, k.astype(BF16), NT) * jnp.exp(d - m_t)
    num = s_inter * _dot(qb, c_prev.astype(BF16)) + _dot(s.astype(BF16), v)
    den = (s_inter * jnp.sum(q * n_prev, axis=1, keepdims=True)
           + jnp.sum(s, axis=1, keepdims=True))
    out = num * (1.0 / jnp.maximum(jnp.abs(den), jnp.exp(-m_t)))
    a = b_last - bc + ic
    m_loc = jnp.max(a, axis=0, keepdims=True)
    kw = k * jnp.exp(a - m_loc)
    m_new = jnp.maximum(b_last + m_prev, m_loc)
    s_old = jnp.exp(b_last + m_prev - m_new)
    s_new = jnp.exp(m_loc - m_new)
    c_ref[...] = s_old * c_prev + s_new * _dotg(kw.astype(BF16), v, TN)
    n_ref[...] = s_old * n_prev + s_new * jnp.sum(kw, axis=0, keepdims=True)
    m_ref[...] = m_new
    return out


def _ml_kernel(q_ref, k_ref, v_ref, og_ref, gcol_ref, grow_ref, cwq_ref, cwk_ref, cbq_ref, cbk_ref,
               ng_ref, o_ref, xq_ref, xk_ref, qc_ref, kc_ref, acc_ref, c_ref, n_ref, m_ref):
    h = pl.program_id(1)
    seq = q_ref.shape[0]
    nc = seq // CHUNK
    scale = ML_DK ** -0.5

    def chunk(c):
        return pl.ds(pl.multiple_of(c * CHUNK, CHUNK), CHUNK)

    zero_halo = jnp.zeros((HALO, ML_DK), F32)
    for ref in (xq_ref, xk_ref):
        ref[0:HALO, :] = zero_halo
        ref[seq + HALO:seq + 2 * HALO, :] = zero_halo

    def stage_body(c, carry):
        dst = pl.ds(pl.multiple_of(c * CHUNK + HALO, HALO), CHUNK)
        xq_ref[dst, :] = q_ref[chunk(c), :].astype(F32)
        xk_ref[dst, :] = k_ref[chunk(c), :].astype(F32)
        return carry

    lax.fori_loop(0, nc, stage_body, 0)

    def conv_body(c, carry):
        wsl = pl.ds(pl.multiple_of(c * CHUNK, CHUNK), CHUNK + 2 * HALO)
        for x_ref, w_ref, b_ref, dst_ref, mul in ((xq_ref, cwq_ref, cbq_ref, qc_ref, scale),
                                                  (xk_ref, cwk_ref, cbk_ref, kc_ref, 1.0)):
            win = x_ref[wsl, :]
            y = b_ref[...] + jnp.zeros((CHUNK, ML_DK), F32)
            for j in range(ML_CONV_W):
                off = HALO - ML_CONV_W // 2 + j
                y = y + win[off:off + CHUNK, :] * w_ref[j:j + 1, :]
            y = y * _sigmoid(y)
            dst_ref[chunk(c), :] = y * mul if mul != 1.0 else y
        return carry

    lax.fori_loop(0, nc, conv_body, 0)

    lane = lax.broadcasted_iota(jnp.int32, (CHUNK, LANES), 1)
    sub = lax.broadcasted_iota(jnp.int32, (N_GATE, CHUNK), 0)
    pi = lax.broadcasted_iota(jnp.int32, (CHUNK, CHUNK), 0)
    pj = lax.broadcasted_iota(jnp.int32, (CHUNK, CHUNK), 1)

    def gate_vectors(c, i_idx, f_idx):
        gc = gcol_ref[chunk(c), :]
        gr = grow_ref[0, :, chunk(c)]
        ic = jnp.sum(jnp.where(lane == i_idx, gc, 0.0), axis=1, keepdims=True)
        bc = jnp.sum(jnp.where(lane == f_idx, gc, 0.0), axis=1, keepdims=True)
        ir = jnp.sum(jnp.where(sub == i_idx, gr, 0.0), axis=0, keepdims=True)
        br = jnp.sum(jnp.where(sub == f_idx, gr, 0.0), axis=0, keepdims=True)
        return ic, bc, ir, br

    def reset_state():
        c_ref[...] = jnp.zeros_like(c_ref)
        n_ref[...] = jnp.zeros_like(n_ref)
        m_ref[...] = jnp.full(m_ref.shape, -jnp.inf, F32)

    reset_state()

    def bwd_body(t, carry):
        c = nc - 1 - t
        sl = chunk(c)
        ic, bc, ir, br = gate_vectors(c, 2 * ML_HEADS + h, 3 * ML_HEADS + h)
        acc_ref[sl, :] = _ml_direction(qc_ref[sl, :], kc_ref[sl, :], v_ref[sl, :], ic, bc, ir, br,
                                       bc[0:1, :], pj >= pi, c_ref, n_ref, m_ref)
        return carry

    lax.fori_loop(0, nc, bwd_body, 0)
    reset_state()

    def fwd_body(c, carry):
        sl = chunk(c)
        ic, bc, ir, br = gate_vectors(c, h, ML_HEADS + h)
        out = _ml_direction(qc_ref[sl, :], kc_ref[sl, :], v_ref[sl, :], ic, bc, ir, br,
                            bc[CHUNK - 1:CHUNK, :], pj <= pi, c_ref, n_ref, m_ref)
        y = _head_layernorm(out + acc_ref[sl, :]) * ng_ref[...]
        o_ref[sl, :] = (_sigmoid(og_ref[sl, :].astype(F32)) * y).astype(o_ref.dtype)
        return carry

    lax.fori_loop(0, nc, fwd_body, 0)


def _mlstm(proj, gcol, grow, conv_w, conv_b, norm_g, batch, seq):
    m = proj.shape[0]
    qb, kb = OFF_MQ // ML_DK, OFF_MK // ML_DK
    vb, ob = OFF_MV // ML_DV, OFF_MO // ML_DV
    kw = conv_w.shape[0]
    return pl.pallas_call(
        _ml_kernel,
        grid=(batch, ML_HEADS),
        in_specs=[
            pl.BlockSpec((seq, ML_DK), lambda b, h: (b, qb + h)),
            pl.BlockSpec((seq, ML_DK), lambda b, h: (b, kb + h)),
            pl.BlockSpec((seq, ML_DV), lambda b, h: (b, vb + h)),
            pl.BlockSpec((seq, ML_DV), lambda b, h: (b, ob + h)),
            pl.BlockSpec((seq, LANES), lambda b, h: (b, 0)),
            pl.BlockSpec((1, N_GATE, seq), lambda b, h: (b, 0, 0)),
            pl.BlockSpec((kw, ML_DK), lambda b, h: (0, h)),
            pl.BlockSpec((kw, ML_DK), lambda b, h: (0, ML_HEADS + h)),
            pl.BlockSpec((1, ML_DK), lambda b, h: (0, h)),
            pl.BlockSpec((1, ML_DK), lambda b, h: (0, ML_HEADS + h)),
            pl.BlockSpec((1, ML_DV), lambda b, h: (0, h)),
        ],
        out_specs=pl.BlockSpec((seq, ML_DV), lambda b, h: (b, h)),
        out_shape=jax.ShapeDtypeStruct((m, ML_V), BF16),
        scratch_shapes=[
            pltpu.VMEM((seq + 2 * HALO, ML_DK), F32),
            pltpu.VMEM((seq + 2 * HALO, ML_DK), F32),
            pltpu.VMEM((seq, ML_DK), F32),
            pltpu.VMEM((seq, ML_DK), F32),
            pltpu.VMEM((seq, ML_DV), F32),
            pltpu.VMEM((ML_DK, ML_DV), F32),
            pltpu.VMEM((1, ML_DK), F32),
            pltpu.VMEM((1, 1), F32),
        ],
        compiler_params=_cparams(("parallel", "arbitrary")),
        name="mlstm",
    )(proj, proj, proj, proj, gcol, grow, conv_w, conv_w, conv_b, conv_b, norm_g)


def _merge_kernel(x_ref, yr_ref, yn_ref, ym_ref, mix_ref, gb_ref, wr_ref, wn_ref, wm_ref, wo_ref,
                  o_ref):
    merged = None
    for i, (y_ref, w_ref) in enumerate(((yr_ref, wr_ref), (yn_ref, wn_ref), (ym_ref, wm_ref))):
        cols = slice(i * D_MODEL, (i + 1) * D_MODEL)
        gate = _sigmoid(mix_ref[:, cols].astype(F32) + gb_ref[:, cols])
        term = gate * _dot(y_ref[...], w_ref[...])
        merged = term if merged is None else merged + term
    o_ref[...] = x_ref[...] + _dot(merged.astype(BF16), wo_ref[...])


def _merge(x2, y_ret, y_na, y_ml, proj, gate_b, w_ret_o, w_na_o, w_ml_o, w_out, tm=512):
    m = x2.shape[0]
    row = lambda i: (i, 0)
    const = lambda i: (0, 0)
    wspec = pl.BlockSpec((D_MODEL, D_MODEL), const)
    return pl.pallas_call(
        _merge_kernel,
        grid=(m // tm,),
        in_specs=[
            pl.BlockSpec((tm, D_MODEL), row),
            pl.BlockSpec((tm, RET_V), row),
            pl.BlockSpec((tm, NA_W), row),
            pl.BlockSpec((tm, ML_V), row),
            pl.BlockSpec((tm, N_BRANCH * D_MODEL), lambda i: (i, OFF_MIX // (N_BRANCH * D_MODEL))),
            pl.BlockSpec((1, N_BRANCH * D_MODEL), const),
            wspec, wspec, wspec, wspec,
        ],
        out_specs=pl.BlockSpec((tm, D_MODEL), row),
        out_shape=jax.ShapeDtypeStruct((m, D_MODEL), F32),
        compiler_params=_cparams(("parallel",)),
        name="merge_out",
    )(x2, y_ret, y_na, y_ml, proj, gate_b, w_ret_o, w_na_o, w_ml_o, w_out)


def _rms(x, g):
    return x * lax.rsqrt(jnp.mean(x * x, axis=-1, keepdims=True) + EPS) * g


def _ffn_kernel(x_ref, g_ref, wu_ref, wd_ref, fg_ref, o_ref, *, final_norm):
    x = x_ref[...]
    h = _rms(x, g_ref[...]).astype(BF16)
    acc = x
    for c in range(D_FF // FF_CHUNK):
        cols = slice(c * FF_CHUNK, (c + 1) * FF_CHUNK)
        a = _dot(h, wu_ref[:, cols])
        u = _dot(h, wu_ref[:, D_FF + c * FF_CHUNK:D_FF + (c + 1) * FF_CHUNK])
        act = (a * _sigmoid(a) * u).astype(BF16)
        acc = acc + _dot(act, wd_ref[cols, :])
    o_ref[...] = _rms(acc, fg_ref[...]) if final_norm else acc


def _ffn(x2, g, w_up, w_down, final_g, final_norm, tm=512):
    m = x2.shape[0]
    const = lambda i: (0, 0)
    return pl.pallas_call(
        functools.partial(_ffn_kernel, final_norm=final_norm),
        grid=(m // tm,),
        in_specs=[
            pl.BlockSpec((tm, D_MODEL), lambda i: (i, 0)),
            pl.BlockSpec((1, D_MODEL), const),
            pl.BlockSpec((D_MODEL, 2 * D_FF), const),
            pl.BlockSpec((D_FF, D_MODEL), const),
            pl.BlockSpec((1, D_MODEL), const),
        ],
        out_specs=pl.BlockSpec((tm, D_MODEL), lambda i: (i, 0)),
        out_shape=jax.ShapeDtypeStruct((m, D_MODEL), F32),
        compiler_params=_cparams(("parallel",)),
        name="ffn",
    )(x2, g, w_up, w_down, final_g)


def kernel(x, norm1_g, w_in, gate_b, ret_decay_logit, na_rpb, ml_conv_w, ml_conv_b, ml_gate_b, ml_norm_g,
           w_ret_o, w_na_o, w_ml_o, w_out, norm2_g, w_ffn_up, w_ffn_down, final_g):
    batch, seq, _ = x.shape
    depth = w_in.shape[0]
    assert seq % (8 * CHUNK) == 0 and seq // GRID_W >= NA_WIN_ROWS

    half = RET_DK // 2
    inv_freq = ROPE_BASE ** (-jnp.arange(half, dtype=F32) / half)
    ang = jnp.arange(seq, dtype=F32)[:, None] * inv_freq[None, :]
    cos, sin = jnp.cos(ang), jnp.sin(ang)
    cos2 = jnp.concatenate([cos, cos], axis=-1)
    sin2 = jnp.concatenate([-sin, sin], axis=-1)

    g0, g1 = W_IN_GATE_OFF, W_IN_GATE_OFF + N_GATE
    w_main = jnp.concatenate([w_in[:, :, :g0], w_in[:, :, g1:]], axis=-1).astype(BF16)
    w_gate = jnp.pad(w_in[:, :, g0:g1], ((0, 0), (0, 0), (0, LANES - N_GATE))).astype(BF16)
    ml_bias = jnp.pad(ml_gate_b.reshape(depth, 1, N_GATE).astype(F32), ((0, 0), (0, 0), (0, LANES - N_GATE)))
    conv_w = jnp.pad(ml_conv_w.astype(F32), ((0, 0), (0, 8 - ML_CONV_W), (0, 0)))
    lg = jax.nn.log_sigmoid(ret_decay_logit.astype(F32))
    w_ret_o, w_na_o, w_ml_o, w_out = (w.astype(BF16) for w in (w_ret_o, w_na_o, w_ml_o, w_out))
    w_ffn_up, w_ffn_down = w_ffn_up.astype(BF16), w_ffn_down.astype(BF16)

    x2 = x.reshape(batch * seq, D_MODEL).astype(F32)
    for l in range(depth):
        proj, pre = _in_proj(x2, norm1_g[l][None], w_main[l], w_gate[l])
        y_ret = _retention(proj, lg[l], cos2, sin2, batch, seq)
        y_na = _neighbourhood(proj, _na_bias_tables(na_rpb[l]), batch, seq)
        gcol, grow = _gates(pre, ml_bias[l], batch, seq)
        y_ml = _mlstm(proj, gcol, grow, conv_w[l], ml_conv_b[l][None].astype(F32),
                      ml_norm_g[l][None].astype(F32), batch, seq)
        x2 = _merge(x2, y_ret, y_na, y_ml, proj, gate_b[l][None].astype(F32),
                    w_ret_o[l], w_na_o[l], w_ml_o[l], w_out[l])
        x2 = _ffn(x2, norm2_g[l][None].astype(F32), w_ffn_up[l], w_ffn_down[l],
                  final_g[None].astype(F32), final_norm=(l == depth - 1))
    return x2.reshape(batch, seq, D_MODEL).astype(x.dtype)
```

```python
import functools

import jax
import jax.numpy as jnp
import numpy as np
from jax import lax
from jax.experimental import pallas as pl
from jax.experimental.pallas import tpu as pltpu

F32 = jnp.float32
BF16 = jnp.bfloat16

D_MODEL = 1024
GRID_W = 64
RET_HEADS, RET_DK, RET_DV = 4, 128, 256
NA_HEADS, NA_DH, NA_WIN_ROWS, NA_WIN_COLS = 16, 64, 8, 16
ML_HEADS, ML_DK, ML_DV, ML_CONV_W = 4, 128, 256, 5
CHUNK = 128
D_FF = 2816
ROPE_BASE = 10000.0
EPS = 1e-6
N_BRANCH = 3

RET_QK = RET_HEADS * RET_DK
RET_V = RET_HEADS * RET_DV
NA_W = NA_HEADS * NA_DH
ML_QK = ML_HEADS * ML_DK
ML_V = ML_HEADS * ML_DV
N_GATE = 4 * ML_HEADS
OFF_RQ = 0
OFF_RK = OFF_RQ + RET_QK
OFF_RV = OFF_RK + RET_QK
OFF_RG = OFF_RV + RET_V
OFF_NQ = OFF_RG + RET_V
OFF_NK = OFF_NQ + NA_W
OFF_NV = OFF_NK + NA_W
OFF_MQ = OFF_NV + NA_W
OFF_MK = OFF_MQ + ML_QK
OFF_MV = OFF_MK + ML_QK
OFF_MO = OFF_MV + ML_V
OFF_MIX = OFF_MO + ML_V
PROJ_W = OFF_MIX + N_BRANCH * D_MODEL
W_IN_GATE_OFF = OFF_MIX
LANES = 128
HALO = 8
FF_CHUNK = 256
RET_UNROLL = 2
ML_UNROLL = 2
NA_UNROLL = 2
VMEM_LIMIT = 56 * 1024 * 1024

NT = (((1,), (1,)), ((), ()))
TN = (((0,), (0,)), ((), ()))


def _dot(a, b):
    return jnp.dot(a, b, preferred_element_type=F32)


def _dotg(a, b, dims):
    return lax.dot_general(a, b, dims, preferred_element_type=F32)


def _sigmoid(x):
    return 1.0 / (1.0 + jnp.exp(-x))


def _cparams(sem):
    return pltpu.CompilerParams(dimension_semantics=sem, vmem_limit_bytes=VMEM_LIMIT)


def _in_proj_kernel(x_ref, g_ref, w_ref, wg_ref, o_ref, og_ref, h_ref):
    @pl.when(pl.program_id(1) == 0)
    def _():
        x = x_ref[...]
        ms = jnp.mean(x * x, axis=-1, keepdims=True)
        h = (x * lax.rsqrt(ms + EPS) * g_ref[...]).astype(BF16)
        h_ref[...] = h
        og_ref[...] = _dot(h, wg_ref[...])

    o_ref[...] = _dot(h_ref[...], w_ref[...]).astype(o_ref.dtype)


def _in_proj(x2, g, w, wg, layer, tm=1024, tn=2048):
    m = x2.shape[0]
    return pl.pallas_call(
        _in_proj_kernel,
        grid=(m // tm, PROJ_W // tn),
        in_specs=[
            pl.BlockSpec((tm, D_MODEL), lambda i, j: (i, 0)),
            pl.BlockSpec((1, D_MODEL), lambda i, j: (0, 0)),
            pl.BlockSpec((None, D_MODEL, tn), lambda i, j: (layer, 0, j)),
            pl.BlockSpec((None, D_MODEL, LANES), lambda i, j: (layer, 0, 0)),
        ],
        out_specs=[
            pl.BlockSpec((tm, tn), lambda i, j: (i, j)),
            pl.BlockSpec((tm, LANES), lambda i, j: (i, 0)),
        ],
        out_shape=[
            jax.ShapeDtypeStruct((m, PROJ_W), BF16),
            jax.ShapeDtypeStruct((m, LANES), F32),
        ],
        scratch_shapes=[pltpu.VMEM((tm, D_MODEL), BF16)],
        compiler_params=_cparams(("parallel", "arbitrary")),
        name="in_proj",
    )(x2, g, w, wg)


def _head_layernorm(o):
    mu = jnp.mean(o, axis=-1, keepdims=True)
    oc = o - mu
    return oc * lax.rsqrt(jnp.mean(oc * oc, axis=-1, keepdims=True) + EPS)


def _ret_kernel(lg_ref, q_ref, k_ref, v_ref, g_ref, cos_ref, sin_ref, o_ref,
                kr_ref, s_ref, r_ref, rf_ref, rb_ref, dw_ref):
    h = pl.program_id(1)
    nc = q_ref.shape[0] // CHUNK
    lgf = lg_ref[0, h]
    lgb = lg_ref[1, h]
    pi = lax.broadcasted_iota(jnp.int32, (CHUNK, CHUNK), 0).astype(F32)
    pj = lax.broadcasted_iota(jnp.int32, (CHUNK, CHUNK), 1).astype(F32)
    diff = pi - pj
    dw_ref[0] = jnp.where(diff >= 0, jnp.exp(lgf * jnp.maximum(diff, 0.0)),
                          jnp.exp(lgb * jnp.maximum(-diff, 0.0)))
    dw_ref[1] = jnp.exp(lgf * (pi + 1.0))
    dw_ref[2] = jnp.exp(lgf * (CHUNK - 1.0 - pi))
    dw_ref[3] = jnp.exp(lgb * (CHUNK - pi))
    dw_ref[4] = jnp.exp(lgb * pi)
    cdf = jnp.exp(jnp.full((1, RET_DV), lgf * CHUNK, F32))
    cdb = jnp.exp(jnp.full((1, RET_DV), lgb * CHUNK, F32))
    scale = RET_DK ** -0.5

    def chunk(c):
        return pl.ds(pl.multiple_of(c * CHUNK, CHUNK), CHUNK)

    def rotary(x_ref, sl):
        x = x_ref[sl, :].astype(F32)
        return x * cos_ref[sl, :] + pltpu.roll(x, RET_DK // 2, 1) * sin_ref[sl, :]

    def state_body(t, carry):
        for u in range(RET_UNROLL):
            c = t * RET_UNROLL + u
            sl = chunk(c)
            kr = rotary(k_ref, sl)
            kr_ref[sl, :] = kr.astype(BF16)
            kcat = jnp.concatenate([(kr * dw_ref[2]).astype(BF16), (kr * dw_ref[4]).astype(BF16)], axis=1)
            s_ref[c] = _dotg(kcat, v_ref[sl, :], TN)
        return carry

    lax.fori_loop(0, nc // RET_UNROLL, state_body, 0)

    rf_ref[...] = jnp.zeros_like(rf_ref)
    rb_ref[...] = jnp.zeros_like(rb_ref)

    def scan_body(t, carry):
        cb = nc - 1 - t
        rf = rf_ref[...]
        rb = rb_ref[...]
        r_ref[t, 0:RET_DK, :] = rf.astype(BF16)
        r_ref[cb, RET_DK:2 * RET_DK, :] = rb.astype(BF16)
        rf_ref[...] = rf * cdf + s_ref[t, 0:RET_DK, :]
        rb_ref[...] = rb * cdb + s_ref[cb, RET_DK:2 * RET_DK, :]
        return carry

    lax.fori_loop(0, nc, scan_body, 0)

    def out_body(t, carry):
        for u in range(RET_UNROLL):
            c = t * RET_UNROLL + u
            sl = chunk(c)
            qr = rotary(q_ref, sl) * scale
            v = v_ref[sl, :]
            s = _dotg(qr.astype(BF16), kr_ref[sl, :], NT) * dw_ref[0]
            qcat = jnp.concatenate([(qr * dw_ref[1]).astype(BF16), (qr * dw_ref[3]).astype(BF16)], axis=1)
            o = _dot(s.astype(BF16), v) + _dot(qcat, r_ref[c])
            g = g_ref[sl, :].astype(F32)
            o_ref[sl, :] = (g * _sigmoid(g) * _head_layernorm(o)).astype(o_ref.dtype)
        return carry

    lax.fori_loop(0, nc // RET_UNROLL, out_body, 0)


def _retention(proj, lg, cos2, sin2, batch, seq):
    m = proj.shape[0]
    nc = seq // CHUNK
    qb, kb = OFF_RQ // RET_DK, OFF_RK // RET_DK
    vb, gb = OFF_RV // RET_DV, OFF_RG // RET_DV
    return pl.pallas_call(
        _ret_kernel,
        grid=(batch, RET_HEADS),
        in_specs=[
            pl.BlockSpec(memory_space=pltpu.SMEM),
            pl.BlockSpec((seq, RET_DK), lambda b, h: (b, qb + h)),
            pl.BlockSpec((seq, RET_DK), lambda b, h: (b, kb + h)),
            pl.BlockSpec((seq, RET_DV), lambda b, h: (b, vb + h)),
            pl.BlockSpec((seq, RET_DV), lambda b, h: (b, gb + h)),
            pl.BlockSpec((seq, RET_DK), lambda b, h: (0, 0)),
            pl.BlockSpec((seq, RET_DK), lambda b, h: (0, 0)),
        ],
        out_specs=pl.BlockSpec((seq, RET_DV), lambda b, h: (b, h)),
        out_shape=jax.ShapeDtypeStruct((m, RET_V), BF16),
        scratch_shapes=[
            pltpu.VMEM((seq, RET_DK), BF16),
            pltpu.VMEM((nc, 2 * RET_DK, RET_DV), F32),
            pltpu.VMEM((nc, 2 * RET_DK, RET_DV), BF16),
            pltpu.VMEM((RET_DK, RET_DV), F32),
            pltpu.VMEM((RET_DK, RET_DV), F32),
            pltpu.VMEM((5, CHUNK, CHUNK), F32),
        ],
        compiler_params=_cparams(("parallel", "arbitrary")),
        name="retention",
    )(lg, proj, proj, proj, proj, cos2, sin2)


def _na_kernel(q_ref, k_ref, v_ref, bias_ref, o_ref, *s_refs):
    rows = q_ref.shape[0] // GRID_W
    band = NA_WIN_ROWS * GRID_W
    win = (NA_WIN_ROWS + 1) * GRID_W
    scale = NA_DH ** -0.5
    head_a = lax.broadcasted_iota(jnp.int32, (GRID_W, LANES), 1) < NA_DH

    def grid_row(r, n):
        return pl.ds(pl.multiple_of(r * GRID_W, GRID_W), n)

    def band_start(r):
        return jnp.clip(r - NA_WIN_ROWS // 2, 0, rows - NA_WIN_ROWS)

    n_pairs = rows // 2

    def window_start(pair):
        return jnp.minimum(band_start(2 * pair), rows - NA_WIN_ROWS - 1)

    def scores(pair, s_ref):
        blocks = []
        for j in range(2):
            q2 = (q_ref[grid_row(2 * pair + j, GRID_W), :].astype(F32) * scale).astype(BF16)
            zero = jnp.zeros_like(q2)
            blocks += [jnp.where(head_a, q2, zero), jnp.where(head_a, zero, q2)]
        qbd = jnp.concatenate(blocks, axis=0)
        s_ref[...] = _dotg(k_ref[grid_row(window_start(pair), win), :], qbd, NT)

    def attend(pair, s_ref):
        ws = window_start(pair)
        for j in range(2):
            r = 2 * pair + j
            rs = band_start(r)
            s = s_ref[grid_row(rs - ws, band), j * LANES:(j + 1) * LANES]
            s = s + bias_ref[grid_row(rs - r + NA_WIN_ROWS - 1, band), :]
            p = jnp.exp(s - jnp.max(s, axis=0, keepdims=True))
            p = p * (1.0 / jnp.sum(p, axis=0, keepdims=True))
            o2 = _dotg(p.astype(BF16), v_ref[grid_row(rs, band), :], TN)
            out = jnp.where(head_a, o2[0:GRID_W, :], o2[GRID_W:2 * GRID_W, :])
            o_ref[grid_row(r, GRID_W), :] = out.astype(o_ref.dtype)

    scores(0, s_refs[0])

    def body(t, carry):
        for u in range(NA_UNROLL):
            pair = t * NA_UNROLL + u
            scores(jnp.minimum(pair + 1, n_pairs - 1), s_refs[(u + 1) % NA_UNROLL])
            attend(pair, s_refs[u])
        return carry

    lax.fori_loop(0, n_pairs // NA_UNROLL, body, 0)


def _na_bias_tables(rpb):
    col = np.arange(GRID_W)
    cs = np.clip(col - NA_WIN_COLS // 2, 0, GRID_W - NA_WIN_COLS)
    kc = np.arange(GRID_W)
    inwin = (kc[:, None] >= cs[None, :]) & (kc[:, None] < cs[None, :] + NA_WIN_COLS)
    col_rel = np.clip(kc[:, None] - col[None, :] + NA_WIN_COLS - 1, 0, 2 * NA_WIN_COLS - 2)
    tab = jnp.where(inwin, rpb.astype(F32)[..., col_rel], -jnp.inf)
    lead = rpb.shape[:-3]
    hp = LANES // NA_DH
    tab = tab.reshape(*lead, NA_HEADS // hp, hp, 2 * NA_WIN_ROWS - 1, GRID_W, GRID_W)
    tab = jnp.moveaxis(tab, -4, -2)
    return tab.reshape(*lead, NA_HEADS // hp, (2 * NA_WIN_ROWS - 1) * GRID_W, LANES)


def _neighbourhood(proj, bias, layer, batch, seq):
    m = proj.shape[0]
    hp = LANES // NA_DH
    qb, kb, vb = OFF_NQ // LANES, OFF_NK // LANES, OFF_NV // LANES
    return pl.pallas_call(
        _na_kernel,
        grid=(NA_HEADS // hp, batch),
        in_specs=[
            pl.BlockSpec((seq, LANES), lambda p, b: (b, qb + p)),
            pl.BlockSpec((seq, LANES), lambda p, b: (b, kb + p)),
            pl.BlockSpec((seq, LANES), lambda p, b: (b, vb + p)),
            pl.BlockSpec((None, None, (2 * NA_WIN_ROWS - 1) * GRID_W, LANES), lambda p, b: (layer, p, 0, 0)),
        ],
        out_specs=pl.BlockSpec((seq, LANES), lambda p, b: (b, p)),
        out_shape=jax.ShapeDtypeStruct((m, NA_W), BF16),
        scratch_shapes=[pltpu.VMEM(((NA_WIN_ROWS + 1) * GRID_W, 2 * LANES), F32)] * NA_UNROLL,
        compiler_params=_cparams(("parallel", "arbitrary")),
        name="neighbourhood",
    )(proj, proj, proj, bias)


def _log_sigmoid(x):
    return jnp.minimum(x, 0.0) - jnp.log1p(jnp.exp(-jnp.abs(x)))


def _split3(x):
    hi = x.astype(BF16)
    r1 = x - hi.astype(F32)
    mid = r1.astype(BF16)
    lo = (r1 - mid.astype(F32)).astype(BF16)
    return hi, mid, lo


def _gate_kernel(pre_ref, b_ref, col_ref, row_ref):
    n_sub = pre_ref.shape[0] // CHUNK
    pi = lax.broadcasted_iota(jnp.int32, (CHUNK, CHUNK), 0)
    pj = lax.broadcasted_iota(jnp.int32, (CHUNK, CHUNK), 1)
    tri = jnp.where(pj <= pi, 1.0, 0.0).astype(BF16)
    lane = lax.broadcasted_iota(jnp.int32, (CHUNK, LANES), 1)
    kind = (lane // ML_HEADS) % 2
    is_fwd_f = (lane < N_GATE) & (kind == 1) & (lane < 2 * ML_HEADS)
    is_bwd_f = (lane < N_GATE) & (kind == 1) & (lane >= 2 * ML_HEADS)
    for s in range(n_sub):
        sl = slice(s * CHUNK, (s + 1) * CHUNK)
        x = pre_ref[sl, :] + b_ref[...]
        ls = _log_sigmoid(x)
        hi, mid, lo = _split3(ls)
        cum = _dot(tri, hi) + _dot(tri, mid) + _dot(tri, lo)
        rev = cum[CHUNK - 1:CHUNK, :] - cum + ls
        out = jnp.where(is_fwd_f, cum, jnp.where(is_bwd_f, rev, x))
        col_ref[sl, :] = out
        row_ref[0, :, sl] = out.T[:N_GATE, :]


def _gates(pre, bias, batch, seq, sub=8):
    m = pre.shape[0]
    rows = sub * CHUNK
    per_b = seq // rows
    return pl.pallas_call(
        _gate_kernel,
        grid=(batch, per_b),
        in_specs=[
            pl.BlockSpec((rows, LANES), lambda b, i: (b * per_b + i, 0)),
            pl.BlockSpec((1, LANES), lambda b, i: (0, 0)),
        ],
        out_specs=[
            pl.BlockSpec((rows, LANES), lambda b, i: (b * per_b + i, 0)),
            pl.BlockSpec((1, N_GATE, rows), lambda b, i: (b, 0, i)),
        ],
        out_shape=[
            jax.ShapeDtypeStruct((m, LANES), F32),
            jax.ShapeDtypeStruct((batch, N_GATE, seq), F32),
        ],
        compiler_params=_cparams(("parallel", "arbitrary")),
        name="mlstm_gates",
    )(pre, bias)


ST_N = 0
ST_M = 2
ST_B = 4


def _ml_kernel(q_ref, k_ref, v_ref, og_ref, gcol_ref, grow_ref, cwq_ref, cwk_ref, cbq_ref, cbk_ref,
               ng_ref, o_ref, xq_ref, xk_ref, qc_ref, kc_ref, cl_ref, cp_ref, sl_ref, sp_ref,
               cst_ref, vst_ref):
    h = pl.program_id(1)
    seq = q_ref.shape[0]
    nc = seq // CHUNK
    scale = ML_DK ** -0.5

    def chunk(c):
        return pl.ds(pl.multiple_of(c * CHUNK, CHUNK), CHUNK)

    zero_halo = jnp.zeros((HALO, ML_DK), F32)
    for ref in (xq_ref, xk_ref):
        ref[0:HALO, :] = zero_halo
        ref[seq + HALO:seq + 2 * HALO, :] = zero_halo

    def stage_body(c, carry):
        dst = pl.ds(pl.multiple_of(c * CHUNK + HALO, HALO), CHUNK)
        xq_ref[dst, :] = q_ref[chunk(c), :].astype(F32)
        xk_ref[dst, :] = k_ref[chunk(c), :].astype(F32)
        return carry

    lax.fori_loop(0, nc, stage_body, 0)

    def conv_body(c, carry):
        wsl = pl.ds(pl.multiple_of(c * CHUNK, CHUNK), CHUNK + 2 * HALO)
        for x_ref, w_ref, b_ref, dst_ref, mul in ((xq_ref, cwq_ref, cbq_ref, qc_ref, scale),
                                                  (xk_ref, cwk_ref, cbk_ref, kc_ref, 1.0)):
            win = x_ref[wsl, :]
            y = b_ref[...] + jnp.zeros((CHUNK, ML_DK), F32)
            for j in range(ML_CONV_W):
                off = HALO - ML_CONV_W // 2 + j
                y = y + win[off:off + CHUNK, :] * w_ref[j:j + 1, :]
            y = y * _sigmoid(y)
            dst_ref[chunk(c), :] = y * mul if mul != 1.0 else y
        return carry

    lax.fori_loop(0, nc, conv_body, 0)

    lane = lax.broadcasted_iota(jnp.int32, (CHUNK, LANES), 1)
    sub = lax.broadcasted_iota(jnp.int32, (N_GATE, CHUNK), 0)
    pi = lax.broadcasted_iota(jnp.int32, (CHUNK, CHUNK), 0)
    pj = lax.broadcasted_iota(jnp.int32, (CHUNK, CHUNK), 1)

    gate_idx = ((h, ML_HEADS + h), (2 * ML_HEADS + h, 3 * ML_HEADS + h))
    last_row = (CHUNK - 1, 0)
    masks = (pj <= pi, pj >= pi)

    def col_vectors(gc, d):
        ic = jnp.sum(jnp.where(lane == gate_idx[d][0], gc, 0.0), axis=1, keepdims=True)
        bc = jnp.sum(jnp.where(lane == gate_idx[d][1], gc, 0.0), axis=1, keepdims=True)
        return ic, bc

    def row_vectors(gr, d):
        ir = jnp.sum(jnp.where(sub == gate_idx[d][0], gr, 0.0), axis=0, keepdims=True)
        br = jnp.sum(jnp.where(sub == gate_idx[d][1], gr, 0.0), axis=0, keepdims=True)
        return ir, br

    def lanes(x):
        return jnp.broadcast_to(x, (1, LANES))

    def local_body(t, carry):
        for u in range(ML_UNROLL):
            c = t * ML_UNROLL + u
            sl = chunk(c)
            k = kc_ref[sl, :]
            gc = gcol_ref[sl, :]
            kws = []
            for d in range(2):
                ic, bc = col_vectors(gc, d)
                b_last = bc[last_row[d]:last_row[d] + 1, :]
                a = b_last - bc + ic
                m_loc = jnp.max(a, axis=0, keepdims=True)
                kw = k * jnp.exp(a - m_loc)
                kws.append(kw.astype(BF16))
                sl_ref[c, ST_N + d:ST_N + d + 1, :] = jnp.sum(kw, axis=0, keepdims=True)
                sl_ref[c, ST_M + d:ST_M + d + 1, :] = lanes(m_loc)
                sl_ref[c, ST_B + d:ST_B + d + 1, :] = lanes(b_last)
            cl_ref[c] = _dotg(jnp.concatenate(kws, axis=1), v_ref[sl, :], TN)
        return carry

    lax.fori_loop(0, nc // ML_UNROLL, local_body, 0)

    cst_ref[...] = jnp.zeros_like(cst_ref)
    vst_ref[...] = jnp.where(lax.broadcasted_iota(jnp.int32, vst_ref.shape, 0) < ST_M, 0.0, -jnp.inf)

    def scan_body(t, carry):
        for d, c in ((0, t), (1, nc - 1 - t)):
            rows = slice(d * ML_DK, (d + 1) * ML_DK)
            loc = sl_ref[c]
            c_old = cst_ref[d]
            n_old = vst_ref[ST_N + d:ST_N + d + 1, :]
            m_old = vst_ref[ST_M + d:ST_M + d + 1, :]
            cp_ref[c, rows, :] = c_old.astype(BF16)
            sp_ref[c, ST_N + d:ST_N + d + 1, :] = n_old
            sp_ref[c, ST_M + d:ST_M + d + 1, :] = m_old
            m_loc = loc[ST_M + d:ST_M + d + 1, :]
            b_last = loc[ST_B + d:ST_B + d + 1, :]
            m_new = jnp.maximum(b_last + m_old, m_loc)
            s_old = jnp.exp(b_last + m_old - m_new)
            s_new = jnp.exp(m_loc - m_new)
            cst_ref[d] = s_old[:, 0:1] * c_old + s_new[:, 0:1] * cl_ref[c, rows, :]
            vst_ref[ST_N + d:ST_N + d + 1, :] = s_old * n_old + s_new * loc[ST_N + d:ST_N + d + 1, :]
            vst_ref[ST_M + d:ST_M + d + 1, :] = m_new
        return carry

    lax.fori_loop(0, nc, scan_body, 0)

    def out_body(t, carry):
        for u in range(ML_UNROLL):
            c = t * ML_UNROLL + u
            sl = chunk(c)
            q = qc_ref[sl, :]
            qb = q.astype(BF16)
            qk = _dotg(qb, kc_ref[sl, :].astype(BF16), NT)
            gc = gcol_ref[sl, :]
            gr = grow_ref[0, :, sl]
            prev = sp_ref[c]
            p_sum = None
            q_parts = []
            for d in range(2):
                ic, bc = col_vectors(gc, d)
                ir, br = row_vectors(gr, d)
                n_prev = prev[ST_N + d:ST_N + d + 1, :]
                m_prev = prev[ST_M + d:ST_M + d + 1, 0:1]
                dlog = jnp.where(masks[d], bc - br + ir, -jnp.inf)
                g = bc + m_prev
                m_t = jnp.maximum(g, jnp.max(dlog, axis=1, keepdims=True))
                s_inter = jnp.exp(g - m_t)
                s = qk * jnp.exp(dlog - m_t)
                den = (s_inter * jnp.sum(q * n_prev, axis=1, keepdims=True)
                       + jnp.sum(s, axis=1, keepdims=True))
                inv = 1.0 / jnp.maximum(jnp.abs(den), jnp.exp(-m_t))
                p_sum = s * inv if p_sum is None else p_sum + s * inv
                q_parts.append((q * (s_inter * inv)).astype(BF16))
            out = _dot(p_sum.astype(BF16), v_ref[sl, :]) + _dot(jnp.concatenate(q_parts, axis=1), cp_ref[c])
            y = _head_layernorm(out) * ng_ref[...]
            o_ref[sl, :] = (_sigmoid(og_ref[sl, :].astype(F32)) * y).astype(o_ref.dtype)
        return carry

    lax.fori_loop(0, nc // ML_UNROLL, out_body, 0)


def _mlstm(proj, gcol, grow, conv_w, conv_b, norm_g, batch, seq):
    m = proj.shape[0]
    nc = seq // CHUNK
    qb, kb = OFF_MQ // ML_DK, OFF_MK // ML_DK
    vb, ob = OFF_MV // ML_DV, OFF_MO // ML_DV
    kw = conv_w.shape[0]
    return pl.pallas_call(
        _ml_kernel,
        grid=(batch, ML_HEADS),
        in_specs=[
            pl.BlockSpec((seq, ML_DK), lambda b, h: (b, qb + h)),
            pl.BlockSpec((seq, ML_DK), lambda b, h: (b, kb + h)),
            pl.BlockSpec((seq, ML_DV), lambda b, h: (b, vb + h)),
            pl.BlockSpec((seq, ML_DV), lambda b, h: (b, ob + h)),
            pl.BlockSpec((seq, LANES), lambda b, h: (b, 0)),
            pl.BlockSpec((1, N_GATE, seq), lambda b, h: (b, 0, 0)),
            pl.BlockSpec((kw, ML_DK), lambda b, h: (0, h)),
            pl.BlockSpec((kw, ML_DK), lambda b, h: (0, ML_HEADS + h)),
            pl.BlockSpec((1, ML_DK), lambda b, h: (0, h)),
            pl.BlockSpec((1, ML_DK), lambda b, h: (0, ML_HEADS + h)),
            pl.BlockSpec((1, ML_DV), lambda b, h: (0, h)),
        ],
        out_specs=pl.BlockSpec((seq, ML_DV), lambda b, h: (b, h)),
        out_shape=jax.ShapeDtypeStruct((m, ML_V), BF16),
        scratch_shapes=[
            pltpu.VMEM((seq + 2 * HALO, ML_DK), F32),
            pltpu.VMEM((seq + 2 * HALO, ML_DK), F32),
            pltpu.VMEM((seq, ML_DK), F32),
            pltpu.VMEM((seq, ML_DK), F32),
            pltpu.VMEM((nc, 2 * ML_DK, ML_DV), F32),
            pltpu.VMEM((nc, 2 * ML_DK, ML_DV), BF16),
            pltpu.VMEM((nc, 8, LANES), F32),
            pltpu.VMEM((nc, 8, LANES), F32),
            pltpu.VMEM((2, ML_DK, ML_DV), F32),
            pltpu.VMEM((8, LANES), F32),
        ],
        compiler_params=_cparams(("parallel", "arbitrary")),
        name="mlstm",
    )(proj, proj, proj, proj, gcol, grow, conv_w, conv_w, conv_b, conv_b, norm_g)


def _merge_kernel(x_ref, yr_ref, yn_ref, ym_ref, mix_ref, gb_ref, wr_ref, wn_ref, wm_ref, wo_ref,
                  o_ref):
    merged = None
    for i, (y_ref, w_ref) in enumerate(((yr_ref, wr_ref), (yn_ref, wn_ref), (ym_ref, wm_ref))):
        cols = slice(i * D_MODEL, (i + 1) * D_MODEL)
        gate = _sigmoid(mix_ref[:, cols].astype(F32) + gb_ref[:, cols])
        term = gate * _dot(y_ref[...], w_ref[...])
        merged = term if merged is None else merged + term
    o_ref[...] = x_ref[...] + _dot(merged.astype(BF16), wo_ref[...])


def _merge(x2, y_ret, y_na, y_ml, proj, gate_b, w_ret_o, w_na_o, w_ml_o, w_out, layer, tm=512):
    m = x2.shape[0]
    row = lambda i: (i, 0)
    const = lambda i: (0, 0)
    wspec = pl.BlockSpec((None, D_MODEL, D_MODEL), lambda i: (layer, 0, 0))
    return pl.pallas_call(
        _merge_kernel,
        grid=(m // tm,),
        in_specs=[
            pl.BlockSpec((tm, D_MODEL), row),
            pl.BlockSpec((tm, RET_V), row),
            pl.BlockSpec((tm, NA_W), row),
            pl.BlockSpec((tm, ML_V), row),
            pl.BlockSpec((tm, N_BRANCH * D_MODEL), lambda i: (i, OFF_MIX // (N_BRANCH * D_MODEL))),
            pl.BlockSpec((1, N_BRANCH * D_MODEL), const),
            wspec, wspec, wspec, wspec,
        ],
        out_specs=pl.BlockSpec((tm, D_MODEL), row),
        out_shape=jax.ShapeDtypeStruct((m, D_MODEL), F32),
        compiler_params=_cparams(("parallel",)),
        name="merge_out",
    )(x2, y_ret, y_na, y_ml, proj, gate_b, w_ret_o, w_na_o, w_ml_o, w_out)


def _rms(x, g):
    return x * lax.rsqrt(jnp.mean(x * x, axis=-1, keepdims=True) + EPS) * g


def _ffn_kernel(x_ref, g_ref, wu_ref, wd_ref, fg_ref, o_ref, *, final_norm):
    x = x_ref[...]
    h = _rms(x, g_ref[...]).astype(BF16)
    acc = x
    for c in range(D_FF // FF_CHUNK):
        cols = slice(c * FF_CHUNK, (c + 1) * FF_CHUNK)
        a = _dot(h, wu_ref[:, cols])
        u = _dot(h, wu_ref[:, D_FF + c * FF_CHUNK:D_FF + (c + 1) * FF_CHUNK])
        act = (a * _sigmoid(a) * u).astype(BF16)
        acc = acc + _dot(act, wd_ref[cols, :])
    o_ref[...] = _rms(acc, fg_ref[...]) if final_norm else acc


def _ffn(x2, g, w_up, w_down, final_g, layer, final_norm, tm=512):
    m = x2.shape[0]
    const = lambda i: (0, 0)
    return pl.pallas_call(
        functools.partial(_ffn_kernel, final_norm=final_norm),
        grid=(m // tm,),
        in_specs=[
            pl.BlockSpec((tm, D_MODEL), lambda i: (i, 0)),
            pl.BlockSpec((1, D_MODEL), const),
            pl.BlockSpec((None, D_MODEL, 2 * D_FF), lambda i: (layer, 0, 0)),
            pl.BlockSpec((None, D_FF, D_MODEL), lambda i: (layer, 0, 0)),
            pl.BlockSpec((1, D_MODEL), const),
        ],
        out_specs=pl.BlockSpec((tm, D_MODEL), lambda i: (i, 0)),
        out_shape=jax.ShapeDtypeStruct((m, D_MODEL), F32),
        compiler_params=_cparams(("parallel",)),
        name="ffn",
    )(x2, g, w_up, w_down, final_g)


def kernel(x, norm1_g, w_in, gate_b, ret_decay_logit, na_rpb, ml_conv_w, ml_conv_b, ml_gate_b, ml_norm_g,
           w_ret_o, w_na_o, w_ml_o, w_out, norm2_g, w_ffn_up, w_ffn_down, final_g):
    batch, seq, _ = x.shape
    depth = w_in.shape[0]
    assert seq % (8 * CHUNK) == 0 and seq // GRID_W >= NA_WIN_ROWS

    half = RET_DK // 2
    inv_freq = ROPE_BASE ** (-jnp.arange(half, dtype=F32) / half)
    ang = jnp.arange(seq, dtype=F32)[:, None] * inv_freq[None, :]
    cos, sin = jnp.cos(ang), jnp.sin(ang)
    cos2 = jnp.concatenate([cos, cos], axis=-1)
    sin2 = jnp.concatenate([-sin, sin], axis=-1)

    g0, g1 = W_IN_GATE_OFF, W_IN_GATE_OFF + N_GATE
    w_main = jnp.concatenate([w_in[:, :, :g0], w_in[:, :, g1:]], axis=-1).astype(BF16)
    w_gate = jnp.pad(w_in[:, :, g0:g1], ((0, 0), (0, 0), (0, LANES - N_GATE))).astype(BF16)
    ml_bias = jnp.pad(ml_gate_b.reshape(depth, 1, N_GATE).astype(F32), ((0, 0), (0, 0), (0, LANES - N_GATE)))
    conv_w = jnp.pad(ml_conv_w.astype(F32), ((0, 0), (0, 8 - ML_CONV_W), (0, 0)))
    lg = jax.nn.log_sigmoid(ret_decay_logit.astype(F32))
    w_ret_o, w_na_o, w_ml_o, w_out = (w.astype(BF16) for w in (w_ret_o, w_na_o, w_ml_o, w_out))
    w_ffn_up, w_ffn_down = w_ffn_up.astype(BF16), w_ffn_down.astype(BF16)

    na_bias = _na_bias_tables(na_rpb)

    x2 = x.reshape(batch * seq, D_MODEL).astype(F32)
    for l in range(depth):
        proj, pre = _in_proj(x2, norm1_g[l][None].astype(F32), w_main, w_gate, l)
        y_ret = _retention(proj, lg[l], cos2, sin2, batch, seq)
        y_na = _neighbourhood(proj, na_bias, l, batch, seq)
        gcol, grow = _gates(pre, ml_bias[l], batch, seq)
        y_ml = _mlstm(proj, gcol, grow, conv_w[l], ml_conv_b[l][None].astype(F32),
                      ml_norm_g[l][None].astype(F32), batch, seq)
        x2 = _merge(x2, y_ret, y_na, y_ml, proj, gate_b[l][None].astype(F32),
                    w_ret_o, w_na_o, w_ml_o, w_out, l)
        x2 = _ffn(x2, norm2_g[l][None].astype(F32), w_ffn_up, w_ffn_down,
                  final_g[None].astype(F32), l, final_norm=(l == depth - 1))
    return x2.reshape(batch, seq, D_MODEL).astype(x.dtype)
```

```python
import functools

import jax
import jax.numpy as jnp
import numpy as np
from jax import lax
from jax.experimental import pallas as pl
from jax.experimental.pallas import tpu as pltpu

F32 = jnp.float32
BF16 = jnp.bfloat16

D_MODEL = 1024
GRID_W = 64
RET_HEADS, RET_DK, RET_DV = 4, 128, 256
NA_HEADS, NA_DH, NA_WIN_ROWS, NA_WIN_COLS = 16, 64, 8, 16
ML_HEADS, ML_DK, ML_DV, ML_CONV_W = 4, 128, 256, 5
CHUNK = 128
D_FF = 2816
ROPE_BASE = 10000.0
LOG2E = 1.4426950408889634
EPS = 1e-6
N_BRANCH = 3

RET_QK = RET_HEADS * RET_DK
RET_V = RET_HEADS * RET_DV
NA_W = NA_HEADS * NA_DH
ML_QK = ML_HEADS * ML_DK
ML_V = ML_HEADS * ML_DV
N_GATE = 4 * ML_HEADS
OFF_RQ = 0
OFF_RK = OFF_RQ + RET_QK
OFF_RV = OFF_RK + RET_QK
OFF_RG = OFF_RV + RET_V
OFF_NQ = OFF_RG + RET_V
OFF_NK = OFF_NQ + NA_W
OFF_NV = OFF_NK + NA_W
OFF_MQ = OFF_NV + NA_W
OFF_MK = OFF_MQ + ML_QK
OFF_MV = OFF_MK + ML_QK
OFF_MO = OFF_MV + ML_V
OFF_MIX = OFF_MO + ML_V
PROJ_W = OFF_MIX + N_BRANCH * D_MODEL
W_IN_GATE_OFF = OFF_MIX
LANES = 128
HALO = 8
FF_CHUNK = 256
RET_UNROLL = 4
ML_UNROLL = 4
NA_UNROLL = 4
VMEM_LIMIT = 56 * 1024 * 1024

NT = (((1,), (1,)), ((), ()))
TN = (((0,), (0,)), ((), ()))


def _dot(a, b):
    return jnp.dot(a, b, preferred_element_type=F32)


def _dotg(a, b, dims):
    return lax.dot_general(a, b, dims, preferred_element_type=F32)


def _sigmoid(x):
    return 1.0 / (1.0 + jnp.exp(-x))


def _cparams(sem):
    return pltpu.CompilerParams(dimension_semantics=sem, vmem_limit_bytes=VMEM_LIMIT)


def _in_proj_kernel(x_ref, g_ref, w_ref, wg_ref, o_ref, og_ref, h_ref):
    @pl.when(pl.program_id(1) == 0)
    def _():
        x = x_ref[...]
        ms = jnp.mean(x * x, axis=-1, keepdims=True)
        h = (x * lax.rsqrt(ms + EPS) * g_ref[...]).astype(BF16)
        h_ref[...] = h
        og_ref[...] = _dot(h, wg_ref[...])

    o_ref[...] = _dot(h_ref[...], w_ref[...]).astype(o_ref.dtype)


def _in_proj(x2, g, w, wg, layer, tm=1024, tn=2048):
    m = x2.shape[0]
    return pl.pallas_call(
        _in_proj_kernel,
        grid=(m // tm, PROJ_W // tn),
        in_specs=[
            pl.BlockSpec((tm, D_MODEL), lambda i, j: (i, 0)),
            pl.BlockSpec((1, D_MODEL), lambda i, j: (0, 0)),
            pl.BlockSpec((None, D_MODEL, tn), lambda i, j: (layer, 0, j)),
            pl.BlockSpec((None, D_MODEL, LANES), lambda i, j: (layer, 0, 0)),
        ],
        out_specs=[
            pl.BlockSpec((tm, tn), lambda i, j: (i, j)),
            pl.BlockSpec((tm, LANES), lambda i, j: (i, 0)),
        ],
        out_shape=[
            jax.ShapeDtypeStruct((m, PROJ_W), BF16),
            jax.ShapeDtypeStruct((m, LANES), F32),
        ],
        scratch_shapes=[pltpu.VMEM((tm, D_MODEL), BF16)],
        compiler_params=_cparams(("parallel", "arbitrary")),
        name="in_proj",
    )(x2, g, w, wg)


def _lane_sum(x):
    return _dot(x.astype(BF16), jnp.ones((x.shape[1], LANES), BF16))


def _head_layernorm(o):
    mu = jnp.mean(o, axis=-1, keepdims=True)
    oc = o - mu
    return oc * lax.rsqrt(jnp.mean(oc * oc, axis=-1, keepdims=True) + EPS)


def _ret_kernel(lg_ref, q_ref, k_ref, v_ref, g_ref, cos_ref, sin_ref, o_ref,
                kr_ref, s_ref, r_ref, rf_ref, rb_ref, dw_ref):
    h = pl.program_id(1)
    nc = q_ref.shape[0] // CHUNK
    lgf = lg_ref[0, h]
    lgb = lg_ref[1, h]
    pi = lax.broadcasted_iota(jnp.int32, (CHUNK, CHUNK), 0).astype(F32)
    pj = lax.broadcasted_iota(jnp.int32, (CHUNK, CHUNK), 1).astype(F32)
    diff = pi - pj
    dw_ref[0] = jnp.where(diff >= 0, jnp.exp(lgf * jnp.maximum(diff, 0.0)),
                          jnp.exp(lgb * jnp.maximum(-diff, 0.0)))
    dw_ref[1] = jnp.exp(lgf * (pi + 1.0))
    dw_ref[2] = jnp.exp(lgf * (CHUNK - 1.0 - pi))
    dw_ref[3] = jnp.exp(lgb * (CHUNK - pi))
    dw_ref[4] = jnp.exp(lgb * pi)
    cdf = jnp.exp(jnp.full((1, RET_DV), lgf * CHUNK, F32))
    cdb = jnp.exp(jnp.full((1, RET_DV), lgb * CHUNK, F32))
    scale = RET_DK ** -0.5

    def chunk(c):
        return pl.ds(pl.multiple_of(c * CHUNK, CHUNK), CHUNK)

    def rotary(x_ref, sl):
        x = x_ref[sl, :].astype(F32)
        return x * cos_ref[sl, :] + pltpu.roll(x, RET_DK // 2, 1) * sin_ref[sl, :]

    def state_body(t, carry):
        for u in range(RET_UNROLL):
            c = t * RET_UNROLL + u
            sl = chunk(c)
            kr = rotary(k_ref, sl)
            kr_ref[sl, :] = kr.astype(BF16)
            kcat = jnp.concatenate([(kr * dw_ref[2]).astype(BF16), (kr * dw_ref[4]).astype(BF16)], axis=1)
            s_ref[c] = _dotg(kcat, v_ref[sl, :], TN)
        return carry

    lax.fori_loop(0, nc // RET_UNROLL, state_body, 0)

    rf_ref[...] = jnp.zeros_like(rf_ref)
    rb_ref[...] = jnp.zeros_like(rb_ref)

    def scan_body(t, carry):
        cb = nc - 1 - t
        rf = rf_ref[...]
        rb = rb_ref[...]
        r_ref[t, 0:RET_DK, :] = rf.astype(BF16)
        r_ref[cb, RET_DK:2 * RET_DK, :] = rb.astype(BF16)
        rf_ref[...] = rf * cdf + s_ref[t, 0:RET_DK, :]
        rb_ref[...] = rb * cdb + s_ref[cb, RET_DK:2 * RET_DK, :]
        return carry

    lax.fori_loop(0, nc, scan_body, 0)

    def out_body(t, carry):
        cs = [t * RET_UNROLL + u for u in range(RET_UNROLL)]
        qrs = [rotary(q_ref, chunk(c)) * scale for c in cs]
        ss = [_dotg(qr.astype(BF16), kr_ref[chunk(c), :], NT) for qr, c in zip(qrs, cs)]
        outs = []
        for qr, s, c in zip(qrs, ss, cs):
            qcat = jnp.concatenate([(qr * dw_ref[1]).astype(BF16), (qr * dw_ref[3]).astype(BF16)], axis=1)
            outs.append(_dot((s * dw_ref[0]).astype(BF16), v_ref[chunk(c), :]) + _dot(qcat, r_ref[c]))
        for o, c in zip(outs, cs):
            g = g_ref[chunk(c), :].astype(F32)
            o_ref[chunk(c), :] = (g * _sigmoid(g) * _head_layernorm(o)).astype(o_ref.dtype)
        return carry

    lax.fori_loop(0, nc // RET_UNROLL, out_body, 0)


def _retention(proj, lg, cos2, sin2, batch, seq):
    m = proj.shape[0]
    nc = seq // CHUNK
    qb, kb = OFF_RQ // RET_DK, OFF_RK // RET_DK
    vb, gb = OFF_RV // RET_DV, OFF_RG // RET_DV
    return pl.pallas_call(
        _ret_kernel,
        grid=(batch, RET_HEADS),
        in_specs=[
            pl.BlockSpec(memory_space=pltpu.SMEM),
            pl.BlockSpec((seq, RET_DK), lambda b, h: (b, qb + h)),
            pl.BlockSpec((seq, RET_DK), lambda b, h: (b, kb + h)),
            pl.BlockSpec((seq, RET_DV), lambda b, h: (b, vb + h)),
            pl.BlockSpec((seq, RET_DV), lambda b, h: (b, gb + h)),
            pl.BlockSpec((seq, RET_DK), lambda b, h: (0, 0)),
            pl.BlockSpec((seq, RET_DK), lambda b, h: (0, 0)),
        ],
        out_specs=pl.BlockSpec((seq, RET_DV), lambda b, h: (b, h)),
        out_shape=jax.ShapeDtypeStruct((m, RET_V), BF16),
        scratch_shapes=[
            pltpu.VMEM((seq, RET_DK), BF16),
            pltpu.VMEM((nc, 2 * RET_DK, RET_DV), F32),
            pltpu.VMEM((nc, 2 * RET_DK, RET_DV), BF16),
            pltpu.VMEM((RET_DK, RET_DV), F32),
            pltpu.VMEM((RET_DK, RET_DV), F32),
            pltpu.VMEM((5, CHUNK, CHUNK), F32),
        ],
        compiler_params=_cparams(("parallel", "arbitrary")),
        name="retention",
    )(lg, proj, proj, proj, proj, cos2, sin2)


def _na_kernel(q_ref, k_ref, v_ref, bias_ref, o_ref, *s_refs):
    rows = q_ref.shape[0] // GRID_W
    band = NA_WIN_ROWS * GRID_W
    win = (NA_WIN_ROWS + 1) * GRID_W
    scale = NA_DH ** -0.5 * LOG2E
    head_a = lax.broadcasted_iota(jnp.int32, (GRID_W, LANES), 1) < NA_DH
    ones = jnp.ones((band, LANES), BF16)

    def grid_row(r, n):
        return pl.ds(pl.multiple_of(r * GRID_W, GRID_W), n)

    def band_start(r):
        return jnp.clip(r - NA_WIN_ROWS // 2, 0, rows - NA_WIN_ROWS)

    n_pairs = rows // 2

    def window_start(pair):
        return jnp.minimum(band_start(2 * pair), rows - NA_WIN_ROWS - 1)

    def scores(pair, s_ref):
        blocks = []
        for j in range(2):
            q2 = (q_ref[grid_row(2 * pair + j, GRID_W), :].astype(F32) * scale).astype(BF16)
            zero = jnp.zeros_like(q2)
            blocks += [jnp.where(head_a, q2, zero), jnp.where(head_a, zero, q2)]
        qbd = jnp.concatenate(blocks, axis=0)
        s_ref[...] = _dotg(k_ref[grid_row(window_start(pair), win), :], qbd, NT)

    def attend(pair, s_ref):
        ws = window_start(pair)
        for j in range(2):
            r = 2 * pair + j
            rs = band_start(r)
            s = s_ref[grid_row(rs - ws, band), j * LANES:(j + 1) * LANES]
            s = s + bias_ref[grid_row(rs - r + NA_WIN_ROWS - 1, band), :]
            p = jnp.exp2(s - jnp.max(s, axis=0, keepdims=True))
            v_aug = jnp.concatenate([v_ref[grid_row(rs, band), :], ones], axis=1)
            o2 = _dotg(p.astype(BF16), v_aug, TN)
            top, bot = o2[0:GRID_W, :], o2[GRID_W:2 * GRID_W, :]
            num = jnp.where(head_a, top[:, 0:LANES], bot[:, 0:LANES])
            den = jnp.where(head_a, top[:, LANES:2 * LANES], bot[:, LANES:2 * LANES])
            o_ref[grid_row(r, GRID_W), :] = (num * (1.0 / den)).astype(o_ref.dtype)

    scores(0, s_refs[0])

    def body(t, carry):
        for u in range(NA_UNROLL):
            pair = t * NA_UNROLL + u
            scores(jnp.minimum(pair + 1, n_pairs - 1), s_refs[(u + 1) % NA_UNROLL])
            attend(pair, s_refs[u])
        return carry

    lax.fori_loop(0, n_pairs // NA_UNROLL, body, 0)


NA_REL_ROWS = 2 * NA_WIN_ROWS - 1
NA_REL_COLS = 2 * NA_WIN_COLS - 1


def _na_bias_kernel(r_ref, o_ref):
    kc = lax.broadcasted_iota(jnp.int32, (GRID_W, LANES), 0)
    lane = lax.broadcasted_iota(jnp.int32, (GRID_W, LANES), 1)
    head_a = lane < GRID_W
    col = jnp.where(head_a, lane, lane - GRID_W)
    cs = jnp.clip(col - NA_WIN_COLS // 2, 0, GRID_W - NA_WIN_COLS)
    in_window = jnp.abs(2 * (kc - cs) - (NA_WIN_COLS - 1)) <= NA_WIN_COLS - 1
    for dr in range(NA_REL_ROWS):
        tiles = []
        for hh in range(LANES // GRID_W):
            row = jnp.broadcast_to(r_ref[hh, dr:dr + 1, :], (GRID_W, LANES))
            shift = (hh * GRID_W - (NA_WIN_COLS - 1)) % LANES
            tiles.append(pltpu.roll(row, shift, 1, stride=1, stride_axis=0))
        o_ref[dr * GRID_W:(dr + 1) * GRID_W, :] = jnp.where(
            in_window, jnp.where(head_a, tiles[0], tiles[1]) * LOG2E, -jnp.inf)


def _na_bias_tables(rpb):
    depth = rpb.shape[0]
    hp = LANES // NA_DH
    rows_pad = -(-NA_REL_ROWS // 8) * 8
    rev = jnp.pad(rpb.astype(F32)[..., ::-1],
                  ((0, 0), (0, 0), (0, rows_pad - NA_REL_ROWS), (0, LANES - NA_REL_COLS)))
    return pl.pallas_call(
        _na_bias_kernel,
        grid=(depth, NA_HEADS // hp),
        in_specs=[pl.BlockSpec((None, hp, rows_pad, LANES), lambda l, p: (l, p, 0, 0))],
        out_specs=pl.BlockSpec((None, None, NA_REL_ROWS * GRID_W, LANES), lambda l, p: (l, p, 0, 0)),
        out_shape=jax.ShapeDtypeStruct((depth, NA_HEADS // hp, NA_REL_ROWS * GRID_W, LANES), F32),
        compiler_params=_cparams(("parallel", "parallel")),
        name="na_bias",
    )(rev)


def _neighbourhood(proj, bias, layer, batch, seq):
    m = proj.shape[0]
    hp = LANES // NA_DH
    qb, kb, vb = OFF_NQ // LANES, OFF_NK // LANES, OFF_NV // LANES
    return pl.pallas_call(
        _na_kernel,
        grid=(NA_HEADS // hp, batch),
        in_specs=[
            pl.BlockSpec((seq, LANES), lambda p, b: (b, qb + p)),
            pl.BlockSpec((seq, LANES), lambda p, b: (b, kb + p)),
            pl.BlockSpec((seq, LANES), lambda p, b: (b, vb + p)),
            pl.BlockSpec((None, None, (2 * NA_WIN_ROWS - 1) * GRID_W, LANES), lambda p, b: (layer, p, 0, 0)),
        ],
        out_specs=pl.BlockSpec((seq, LANES), lambda p, b: (b, p)),
        out_shape=jax.ShapeDtypeStruct((m, NA_W), BF16),
        scratch_shapes=[pltpu.VMEM(((NA_WIN_ROWS + 1) * GRID_W, 2 * LANES), F32)] * NA_UNROLL,
        compiler_params=_cparams(("parallel", "arbitrary")),
        name="neighbourhood",
    )(proj, proj, proj, bias)


def _log_sigmoid(x):
    return jnp.minimum(x, 0.0) - jnp.log1p(jnp.exp(-jnp.abs(x)))


def _split3(x):
    hi = x.astype(BF16)
    r1 = x - hi.astype(F32)
    mid = r1.astype(BF16)
    lo = (r1 - mid.astype(F32)).astype(BF16)
    return hi, mid, lo


def _gate_kernel(pre_ref, b_ref, col_ref, row_ref):
    n_sub = pre_ref.shape[0] // CHUNK
    pi = lax.broadcasted_iota(jnp.int32, (CHUNK, CHUNK), 0)
    pj = lax.broadcasted_iota(jnp.int32, (CHUNK, CHUNK), 1)
    tri = jnp.where(pj <= pi, 1.0, 0.0).astype(BF16)
    lane = lax.broadcasted_iota(jnp.int32, (CHUNK, LANES), 1)
    kind = (lane // ML_HEADS) % 2
    is_fwd_f = (lane < N_GATE) & (kind == 1) & (lane < 2 * ML_HEADS)
    is_bwd_f = (lane < N_GATE) & (kind == 1) & (lane >= 2 * ML_HEADS)
    for s in range(n_sub):
        sl = slice(s * CHUNK, (s + 1) * CHUNK)
        x = pre_ref[sl, :] + b_ref[...]
        ls = _log_sigmoid(x)
        hi, mid, lo = _split3(ls)
        cum = _dot(tri, hi) + _dot(tri, mid) + _dot(tri, lo)
        rev = cum[CHUNK - 1:CHUNK, :] - cum + ls
        out = jnp.where(is_fwd_f, cum, jnp.where(is_bwd_f, rev, x))
        col_ref[sl, :] = out
        row_ref[0, :, sl] = out.T[:N_GATE, :]


def _gates(pre, bias, batch, seq, sub=8):
    m = pre.shape[0]
    rows = sub * CHUNK
    per_b = seq // rows
    return pl.pallas_call(
        _gate_kernel,
        grid=(batch, per_b),
        in_specs=[
            pl.BlockSpec((rows, LANES), lambda b, i: (b * per_b + i, 0)),
            pl.BlockSpec((1, LANES), lambda b, i: (0, 0)),
        ],
        out_specs=[
            pl.BlockSpec((rows, LANES), lambda b, i: (b * per_b + i, 0)),
            pl.BlockSpec((1, N_GATE, rows), lambda b, i: (b, 0, i)),
        ],
        out_shape=[
            jax.ShapeDtypeStruct((m, LANES), F32),
            jax.ShapeDtypeStruct((batch, N_GATE, seq), F32),
        ],
        compiler_params=_cparams(("parallel", "arbitrary")),
        name="mlstm_gates",
    )(pre, bias)


ST_N = 0
ST_M = 2
ST_B = 4


def _ml_kernel(q_ref, k_ref, v_ref, og_ref, gcol_ref, grow_ref, cwq_ref, cwk_ref, cbq_ref, cbk_ref,
               ng_ref, o_ref, xq_ref, xk_ref, qc_ref, kc_ref, cl_ref, cp_ref, sl_ref, sp_ref,
               cst_ref, vst_ref):
    h = pl.program_id(1)
    seq = q_ref.shape[0]
    nc = seq // CHUNK
    scale = ML_DK ** -0.5

    def chunk(c):
        return pl.ds(pl.multiple_of(c * CHUNK, CHUNK), CHUNK)

    zero_halo = jnp.zeros((HALO, ML_DK), F32)
    for ref in (xq_ref, xk_ref):
        ref[0:HALO, :] = zero_halo
        ref[seq + HALO:seq + 2 * HALO, :] = zero_halo

    def stage_body(c, carry):
        dst = pl.ds(pl.multiple_of(c * CHUNK + HALO, HALO), CHUNK)
        xq_ref[dst, :] = q_ref[chunk(c), :].astype(F32)
        xk_ref[dst, :] = k_ref[chunk(c), :].astype(F32)
        return carry

    lax.fori_loop(0, nc, stage_body, 0)

    def conv_body(c, carry):
        wsl = pl.ds(pl.multiple_of(c * CHUNK, CHUNK), CHUNK + 2 * HALO)
        for x_ref, w_ref, b_ref, dst_ref, mul in ((xq_ref, cwq_ref, cbq_ref, qc_ref, scale),
                                                  (xk_ref, cwk_ref, cbk_ref, kc_ref, 1.0)):
            win = x_ref[wsl, :]
            y = b_ref[...] + jnp.zeros((CHUNK, ML_DK), F32)
            for j in range(ML_CONV_W):
                off = HALO - ML_CONV_W // 2 + j
                y = y + win[off:off + CHUNK, :] * w_ref[j:j + 1, :]
            y = y * _sigmoid(y)
            dst_ref[chunk(c), :] = y * mul if mul != 1.0 else y
        return carry

    lax.fori_loop(0, nc, conv_body, 0)

    lane = lax.broadcasted_iota(jnp.int32, (CHUNK, LANES), 1)
    sub = lax.broadcasted_iota(jnp.int32, (N_GATE, CHUNK), 0)
    pi = lax.broadcasted_iota(jnp.int32, (CHUNK, CHUNK), 0)
    pj = lax.broadcasted_iota(jnp.int32, (CHUNK, CHUNK), 1)

    gate_idx = ((h, ML_HEADS + h), (2 * ML_HEADS + h, 3 * ML_HEADS + h))
    last_row = (CHUNK - 1, 0)
    masks = (pj <= pi, pj >= pi)

    def col_vectors(gc, d):
        ic = jnp.sum(jnp.where(lane == gate_idx[d][0], gc, 0.0), axis=1, keepdims=True)
        bc = jnp.sum(jnp.where(lane == gate_idx[d][1], gc, 0.0), axis=1, keepdims=True)
        return ic, bc

    def row_vectors(gr, d):
        ir = jnp.sum(jnp.where(sub == gate_idx[d][0], gr, 0.0), axis=0, keepdims=True)
        br = jnp.sum(jnp.where(sub == gate_idx[d][1], gr, 0.0), axis=0, keepdims=True)
        return ir, br

    def lanes(x):
        return jnp.broadcast_to(x, (1, LANES))

    def local_body(t, carry):
        for u in range(ML_UNROLL):
            c = t * ML_UNROLL + u
            sl = chunk(c)
            k = kc_ref[sl, :]
            gc = gcol_ref[sl, :]
            kws = []
            for d in range(2):
                ic, bc = col_vectors(gc, d)
                b_last = bc[last_row[d]:last_row[d] + 1, :]
                a = b_last - bc + ic
                m_loc = jnp.max(a, axis=0, keepdims=True)
                kw = k * jnp.exp(a - m_loc)
                kws.append(kw.astype(BF16))
                sl_ref[c, ST_N + d:ST_N + d + 1, :] = jnp.sum(kw, axis=0, keepdims=True)
                sl_ref[c, ST_M + d:ST_M + d + 1, :] = lanes(m_loc)
                sl_ref[c, ST_B + d:ST_B + d + 1, :] = lanes(b_last)
            cl_ref[c] = _dotg(jnp.concatenate(kws, axis=1), v_ref[sl, :], TN)
        return carry

    lax.fori_loop(0, nc // ML_UNROLL, local_body, 0)

    cst_ref[...] = jnp.zeros_like(cst_ref)
    vst_ref[...] = jnp.where(lax.broadcasted_iota(jnp.int32, vst_ref.shape, 0) < ST_M, 0.0, -jnp.inf)

    def scan_body(t, carry):
        for d, c in ((0, t), (1, nc - 1 - t)):
            rows = slice(d * ML_DK, (d + 1) * ML_DK)
            loc = sl_ref[c]
            c_old = cst_ref[d]
            n_old = vst_ref[ST_N + d:ST_N + d + 1, :]
            m_old = vst_ref[ST_M + d:ST_M + d + 1, :]
            cp_ref[c, rows, :] = c_old.astype(BF16)
            sp_ref[c, ST_N + d:ST_N + d + 1, :] = n_old
            sp_ref[c, ST_M + d:ST_M + d + 1, :] = m_old
            m_loc = loc[ST_M + d:ST_M + d + 1, :]
            b_last = loc[ST_B + d:ST_B + d + 1, :]
            m_new = jnp.maximum(b_last + m_old, m_loc)
            s_old = jnp.exp(b_last + m_old - m_new)
            s_new = jnp.exp(m_loc - m_new)
            cst_ref[d] = s_old[:, 0:1] * c_old + s_new[:, 0:1] * cl_ref[c, rows, :]
            vst_ref[ST_N + d:ST_N + d + 1, :] = s_old * n_old + s_new * loc[ST_N + d:ST_N + d + 1, :]
            vst_ref[ST_M + d:ST_M + d + 1, :] = m_new
        return carry

    lax.fori_loop(0, nc, scan_body, 0)

    def out_body(t, carry):
        cs = [t * ML_UNROLL + u for u in range(ML_UNROLL)]
        qs = [qc_ref[chunk(c), :] for c in cs]
        prevs = [sp_ref[c] for c in cs]
        qks, qns = [], []
        for q, prev, c in zip(qs, prevs, cs):
            qb = q.astype(BF16)
            qks.append(_dotg(qb, kc_ref[chunk(c), :].astype(BF16), NT))
            n_rows = jnp.concatenate(
                [jnp.broadcast_to(prev[ST_N + d:ST_N + d + 1, :], (CHUNK, ML_DK)) for d in range(2)], axis=0)
            qns.append(_dotg(qb, n_rows.astype(BF16), NT))
        stats = []
        for qk, prev, c in zip(qks, prevs, cs):
            gc = gcol_ref[chunk(c), :]
            gr = grow_ref[0, :, chunk(c)]
            per_dir = []
            for d in range(2):
                _, bc = col_vectors(gc, d)
                ir, br = row_vectors(gr, d)
                m_prev = prev[ST_M + d:ST_M + d + 1, 0:1]
                dlog = jnp.where(masks[d], bc - br + ir, -jnp.inf)
                g = bc + m_prev
                m_t = jnp.maximum(g, jnp.max(dlog, axis=1, keepdims=True))
                s = qk * jnp.exp(dlog - m_t)
                per_dir.append((s, _lane_sum(s), jnp.exp(g - m_t), jnp.exp(-m_t)))
            stats.append(per_dir)
        outs = []
        for q, qn, per_dir, c in zip(qs, qns, stats, cs):
            p_sum = None
            q_parts = []
            for d, (s, s_sum, s_inter, floor) in enumerate(per_dir):
                den = s_inter * qn[:, d * LANES:(d + 1) * LANES] + s_sum
                inv = 1.0 / jnp.maximum(jnp.abs(den), floor)
                p_sum = s * inv if p_sum is None else p_sum + s * inv
                q_parts.append((q * (s_inter * inv)).astype(BF16))
            outs.append(_dot(p_sum.astype(BF16), v_ref[chunk(c), :])
                        + _dot(jnp.concatenate(q_parts, axis=1), cp_ref[c]))
        for out, c in zip(outs, cs):
            y = _head_layernorm(out) * ng_ref[...]
            o_ref[chunk(c), :] = (_sigmoid(og_ref[chunk(c), :].astype(F32)) * y).astype(o_ref.dtype)
        return carry

    lax.fori_loop(0, nc // ML_UNROLL, out_body, 0)


def _mlstm(proj, gcol, grow, conv_w, conv_b, norm_g, batch, seq):
    m = proj.shape[0]
    nc = seq // CHUNK
    qb, kb = OFF_MQ // ML_DK, OFF_MK // ML_DK
    vb, ob = OFF_MV // ML_DV, OFF_MO // ML_DV
    kw = conv_w.shape[0]
    return pl.pallas_call(
        _ml_kernel,
        grid=(batch, ML_HEADS),
        in_specs=[
            pl.BlockSpec((seq, ML_DK), lambda b, h: (b, qb + h)),
            pl.BlockSpec((seq, ML_DK), lambda b, h: (b, kb + h)),
            pl.BlockSpec((seq, ML_DV), lambda b, h: (b, vb + h)),
            pl.BlockSpec((seq, ML_DV), lambda b, h: (b, ob + h)),
            pl.BlockSpec((seq, LANES), lambda b, h: (b, 0)),
            pl.BlockSpec((1, N_GATE, seq), lambda b, h: (b, 0, 0)),
            pl.BlockSpec((kw, ML_DK), lambda b, h: (0, h)),
            pl.BlockSpec((kw, ML_DK), lambda b, h: (0, ML_HEADS + h)),
            pl.BlockSpec((1, ML_DK), lambda b, h: (0, h)),
            pl.BlockSpec((1, ML_DK), lambda b, h: (0, ML_HEADS + h)),
            pl.BlockSpec((1, ML_DV), lambda b, h: (0, h)),
        ],
        out_specs=pl.BlockSpec((seq, ML_DV), lambda b, h: (b, h)),
        out_shape=jax.ShapeDtypeStruct((m, ML_V), BF16),
        scratch_shapes=[
            pltpu.VMEM((seq + 2 * HALO, ML_DK), F32),
            pltpu.VMEM((seq + 2 * HALO, ML_DK), F32),
            pltpu.VMEM((seq, ML_DK), F32),
            pltpu.VMEM((seq, ML_DK), F32),
            pltpu.VMEM((nc, 2 * ML_DK, ML_DV), F32),
            pltpu.VMEM((nc, 2 * ML_DK, ML_DV), BF16),
            pltpu.VMEM((nc, 8, LANES), F32),
            pltpu.VMEM((nc, 8, LANES), F32),
            pltpu.VMEM((2, ML_DK, ML_DV), F32),
            pltpu.VMEM((8, LANES), F32),
        ],
        compiler_params=_cparams(("parallel", "arbitrary")),
        name="mlstm",
    )(proj, proj, proj, proj, gcol, grow, conv_w, conv_w, conv_b, conv_b, norm_g)


def _merge_kernel(x_ref, yr_ref, yn_ref, ym_ref, mix_ref, gb_ref, wr_ref, wn_ref, wm_ref, wo_ref,
                  o_ref):
    merged = None
    for i, (y_ref, w_ref) in enumerate(((yr_ref, wr_ref), (yn_ref, wn_ref), (ym_ref, wm_ref))):
        cols = slice(i * D_MODEL, (i + 1) * D_MODEL)
        gate = _sigmoid(mix_ref[:, cols].astype(F32) + gb_ref[:, cols])
        term = gate * _dot(y_ref[...], w_ref[...])
        merged = term if merged is None else merged + term
    o_ref[...] = x_ref[...] + _dot(merged.astype(BF16), wo_ref[...])


def _merge(x2, y_ret, y_na, y_ml, proj, gate_b, w_ret_o, w_na_o, w_ml_o, w_out, layer, tm=512):
    m = x2.shape[0]
    row = lambda i: (i, 0)
    const = lambda i: (0, 0)
    wspec = pl.BlockSpec((None, D_MODEL, D_MODEL), lambda i: (layer, 0, 0))
    return pl.pallas_call(
        _merge_kernel,
        grid=(m // tm,),
        in_specs=[
            pl.BlockSpec((tm, D_MODEL), row),
            pl.BlockSpec((tm, RET_V), row),
            pl.BlockSpec((tm, NA_W), row),
            pl.BlockSpec((tm, ML_V), row),
            pl.BlockSpec((tm, N_BRANCH * D_MODEL), lambda i: (i, OFF_MIX // (N_BRANCH * D_MODEL))),
            pl.BlockSpec((1, N_BRANCH * D_MODEL), const),
            wspec, wspec, wspec, wspec,
        ],
        out_specs=pl.BlockSpec((tm, D_MODEL), row),
        out_shape=jax.ShapeDtypeStruct((m, D_MODEL), F32),
        compiler_params=_cparams(("parallel",)),
        name="merge_out",
    )(x2, y_ret, y_na, y_ml, proj, gate_b, w_ret_o, w_na_o, w_ml_o, w_out)


def _rms(x, g):
    return x * lax.rsqrt(jnp.mean(x * x, axis=-1, keepdims=True) + EPS) * g


def _ffn_kernel(x_ref, g_ref, wu_ref, wd_ref, fg_ref, o_ref, *, final_norm):
    x = x_ref[...]
    h = _rms(x, g_ref[...]).astype(BF16)
    acc = x
    for c in range(D_FF // FF_CHUNK):
        cols = slice(c * FF_CHUNK, (c + 1) * FF_CHUNK)
        a = _dot(h, wu_ref[:, cols])
        u = _dot(h, wu_ref[:, D_FF + c * FF_CHUNK:D_FF + (c + 1) * FF_CHUNK])
        act = (a * _sigmoid(a) * u).astype(BF16)
        acc = acc + _dot(act, wd_ref[cols, :])
    o_ref[...] = _rms(acc, fg_ref[...]) if final_norm else acc


def _ffn(x2, g, w_up, w_down, final_g, layer, final_norm, tm=512):
    m = x2.shape[0]
    const = lambda i: (0, 0)
    return pl.pallas_call(
        functools.partial(_ffn_kernel, final_norm=final_norm),
        grid=(m // tm,),
        in_specs=[
            pl.BlockSpec((tm, D_MODEL), lambda i: (i, 0)),
            pl.BlockSpec((1, D_MODEL), const),
            pl.BlockSpec((None, D_MODEL, 2 * D_FF), lambda i: (layer, 0, 0)),
            pl.BlockSpec((None, D_FF, D_MODEL), lambda i: (layer, 0, 0)),
            pl.BlockSpec((1, D_MODEL), const),
        ],
        out_specs=pl.BlockSpec((tm, D_MODEL), lambda i: (i, 0)),
        out_shape=jax.ShapeDtypeStruct((m, D_MODEL), F32),
        compiler_params=_cparams(("parallel",)),
        name="ffn",
    )(x2, g, w_up, w_down, final_g)


def kernel(x, norm1_g, w_in, gate_b, ret_decay_logit, na_rpb, ml_conv_w, ml_conv_b, ml_gate_b, ml_norm_g,
           w_ret_o, w_na_o, w_ml_o, w_out, norm2_g, w_ffn_up, w_ffn_down, final_g):
    batch, seq, _ = x.shape
    depth = w_in.shape[0]
    assert seq % (8 * CHUNK) == 0 and seq // GRID_W >= NA_WIN_ROWS

    half = RET_DK // 2
    inv_freq = ROPE_BASE ** (-jnp.arange(half, dtype=F32) / half)
    ang = jnp.arange(seq, dtype=F32)[:, None] * inv_freq[None, :]
    cos, sin = jnp.cos(ang), jnp.sin(ang)
    cos2 = jnp.concatenate([cos, cos], axis=-1)
    sin2 = jnp.concatenate([-sin, sin], axis=-1)

    g0, g1 = W_IN_GATE_OFF, W_IN_GATE_OFF + N_GATE
    w_main = jnp.concatenate([w_in[:, :, :g0], w_in[:, :, g1:]], axis=-1).astype(BF16)
    w_gate = jnp.pad(w_in[:, :, g0:g1], ((0, 0), (0, 0), (0, LANES - N_GATE))).astype(BF16)
    ml_bias = jnp.pad(ml_gate_b.reshape(depth, 1, N_GATE).astype(F32), ((0, 0), (0, 0), (0, LANES - N_GATE)))
    conv_w = jnp.pad(ml_conv_w.astype(F32), ((0, 0), (0, 8 - ML_CONV_W), (0, 0)))
    lg = jax.nn.log_sigmoid(ret_decay_logit.astype(F32))
    w_ret_o, w_na_o, w_ml_o, w_out = (w.astype(BF16) for w in (w_ret_o, w_na_o, w_ml_o, w_out))
    w_ffn_up, w_ffn_down = w_ffn_up.astype(BF16), w_ffn_down.astype(BF16)

    na_bias = _na_bias_tables(na_rpb)

    x2 = x.reshape(batch * seq, D_MODEL).astype(F32)
    for l in range(depth):
        proj, pre = _in_proj(x2, norm1_g[l][None].astype(F32), w_main, w_gate, l)
        y_ret = _retention(proj, lg[l], cos2, sin2, batch, seq)
        y_na = _neighbourhood(proj, na_bias, l, batch, seq)
        gcol, grow = _gates(pre, ml_bias[l], batch, seq)
        y_ml = _mlstm(proj, gcol, grow, conv_w[l], ml_conv_b[l][None].astype(F32),
                      ml_norm_g[l][None].astype(F32), batch, seq)
        x2 = _merge(x2, y_ret, y_na, y_ml, proj, gate_b[l][None].astype(F32),
                    w_ret_o, w_na_o, w_ml_o, w_out, l)
        x2 = _ffn(x2, norm2_g[l][None].astype(F32), w_ffn_up, w_ffn_down,
                  final_g[None].astype(F32), l, final_norm=(l == depth - 1))
    return x2.reshape(batch, seq, D_MODEL).astype(x.dtype)
```

```python
import functools

import jax
import jax.numpy as jnp
import numpy as np
from jax import lax
from jax.experimental import pallas as pl
from jax.experimental.pallas import tpu as pltpu

F32 = jnp.float32
BF16 = jnp.bfloat16

D_MODEL = 1024
GRID_W = 64
RET_HEADS, RET_DK, RET_DV = 4, 128, 256
NA_HEADS, NA_DH, NA_WIN_ROWS, NA_WIN_COLS = 16, 64, 8, 16
ML_HEADS, ML_DK, ML_DV, ML_CONV_W = 4, 128, 256, 5
CHUNK = 128
D_FF = 2816
ROPE_BASE = 10000.0
LOG2E = 1.4426950408889634
EPS = 1e-6
N_BRANCH = 3

RET_QK = RET_HEADS * RET_DK
RET_V = RET_HEADS * RET_DV
NA_W = NA_HEADS * NA_DH
ML_QK = ML_HEADS * ML_DK
ML_V = ML_HEADS * ML_DV
N_GATE = 4 * ML_HEADS
OFF_RQ = 0
OFF_RK = OFF_RQ + RET_QK
OFF_RV = OFF_RK + RET_QK
OFF_RG = OFF_RV + RET_V
OFF_NQ = OFF_RG + RET_V
OFF_NK = OFF_NQ + NA_W
OFF_NV = OFF_NK + NA_W
OFF_MQ = OFF_NV + NA_W
OFF_MK = OFF_MQ + ML_QK
OFF_MV = OFF_MK + ML_QK
OFF_MO = OFF_MV + ML_V
OFF_MIX = OFF_MO + ML_V
PROJ_W = OFF_MIX + N_BRANCH * D_MODEL
W_IN_GATE_OFF = OFF_MIX
LANES = 128
HALO = 8
FF_CHUNK = 256
RET_UNROLL = 4
ML_UNROLL = 4
NA_UNROLL = 4
VMEM_LIMIT = 56 * 1024 * 1024

NT = (((1,), (1,)), ((), ()))
TN = (((0,), (0,)), ((), ()))


def _dot(a, b):
    return jnp.dot(a, b, preferred_element_type=F32)


def _dotg(a, b, dims):
    return lax.dot_general(a, b, dims, preferred_element_type=F32)


def _sigmoid(x):
    return 1.0 / (1.0 + jnp.exp(-x))


def _cparams(sem):
    return pltpu.CompilerParams(dimension_semantics=sem, vmem_limit_bytes=VMEM_LIMIT)


def _in_proj_kernel(x_ref, g_ref, w_ref, wg_ref, o_ref, og_ref, h_ref):
    @pl.when(pl.program_id(1) == 0)
    def _():
        x = x_ref[...]
        ms = jnp.mean(x * x, axis=-1, keepdims=True)
        h = (x * lax.rsqrt(ms + EPS) * g_ref[...]).astype(BF16)
        h_ref[...] = h
        og_ref[...] = _dot(h, wg_ref[...])

    o_ref[...] = _dot(h_ref[...], w_ref[...]).astype(o_ref.dtype)


def _in_proj(x2, g, w, wg, layer, tm=1024, tn=2048):
    m = x2.shape[0]
    return pl.pallas_call(
        _in_proj_kernel,
        grid=(m // tm, PROJ_W // tn),
        in_specs=[
            pl.BlockSpec((tm, D_MODEL), lambda i, j: (i, 0)),
            pl.BlockSpec((1, D_MODEL), lambda i, j: (0, 0)),
            pl.BlockSpec((None, D_MODEL, tn), lambda i, j: (layer, 0, j)),
            pl.BlockSpec((None, D_MODEL, LANES), lambda i, j: (layer, 0, 0)),
        ],
        out_specs=[
            pl.BlockSpec((tm, tn), lambda i, j: (i, j)),
            pl.BlockSpec((tm, LANES), lambda i, j: (i, 0)),
        ],
        out_shape=[
            jax.ShapeDtypeStruct((m, PROJ_W), BF16),
            jax.ShapeDtypeStruct((m, LANES), F32),
        ],
        scratch_shapes=[pltpu.VMEM((tm, D_MODEL), BF16)],
        compiler_params=_cparams(("parallel", "arbitrary")),
        name="in_proj",
    )(x2, g, w, wg)


def _lane_sum(x):
    return _dot(x.astype(BF16), jnp.ones((x.shape[1], LANES), BF16))


def _head_layernorm(o):
    mu = jnp.mean(o, axis=-1, keepdims=True)
    oc = o - mu
    return oc * lax.rsqrt(jnp.mean(oc * oc, axis=-1, keepdims=True) + EPS)


def _ret_kernel(lg_ref, q_ref, k_ref, v_ref, g_ref, cos_ref, sin_ref, o_ref,
                kr_ref, s_ref, r_ref, rf_ref, rb_ref, dw_ref):
    h = pl.program_id(1)
    nc = q_ref.shape[0] // CHUNK
    lgf = lg_ref[0, h]
    lgb = lg_ref[1, h]
    pi = lax.broadcasted_iota(jnp.int32, (CHUNK, CHUNK), 0).astype(F32)
    pj = lax.broadcasted_iota(jnp.int32, (CHUNK, CHUNK), 1).astype(F32)
    diff = pi - pj
    dw_ref[0] = jnp.where(diff >= 0, jnp.exp(lgf * jnp.maximum(diff, 0.0)),
                          jnp.exp(lgb * jnp.maximum(-diff, 0.0)))
    dw_ref[1] = jnp.exp(lgf * (pi + 1.0))
    dw_ref[2] = jnp.exp(lgf * (CHUNK - 1.0 - pi))
    dw_ref[3] = jnp.exp(lgb * (CHUNK - pi))
    dw_ref[4] = jnp.exp(lgb * pi)
    cdf = jnp.exp(jnp.full((1, RET_DV), lgf * CHUNK, F32))
    cdb = jnp.exp(jnp.full((1, RET_DV), lgb * CHUNK, F32))
    scale = RET_DK ** -0.5

    def chunk(c):
        return pl.ds(pl.multiple_of(c * CHUNK, CHUNK), CHUNK)

    def rotary(x_ref, sl):
        x = x_ref[sl, :].astype(F32)
        return x * cos_ref[sl, :] + pltpu.roll(x, RET_DK // 2, 1) * sin_ref[sl, :]

    def state_body(t, carry):
        for u in range(RET_UNROLL):
            c = t * RET_UNROLL + u
            sl = chunk(c)
            kr = rotary(k_ref, sl)
            kr_ref[sl, :] = kr.astype(BF16)
            kcat = jnp.concatenate([(kr * dw_ref[2]).astype(BF16), (kr * dw_ref[4]).astype(BF16)], axis=1)
            s_ref[c] = _dotg(kcat, v_ref[sl, :], TN)
        return carry

    lax.fori_loop(0, nc // RET_UNROLL, state_body, 0)

    rf_ref[...] = jnp.zeros_like(rf_ref)
    rb_ref[...] = jnp.zeros_like(rb_ref)

    def scan_body(t, carry):
        cb = nc - 1 - t
        rf = rf_ref[...]
        rb = rb_ref[...]
        r_ref[t, 0:RET_DK, :] = rf.astype(BF16)
        r_ref[cb, RET_DK:2 * RET_DK, :] = rb.astype(BF16)
        rf_ref[...] = rf * cdf + s_ref[t, 0:RET_DK, :]
        rb_ref[...] = rb * cdb + s_ref[cb, RET_DK:2 * RET_DK, :]
        return carry

    lax.fori_loop(0, nc, scan_body, 0)

    def out_body(t, carry):
        cs = [t * RET_UNROLL + u for u in range(RET_UNROLL)]
        qrs = [rotary(q_ref, chunk(c)) * scale for c in cs]
        ss = [_dotg(qr.astype(BF16), kr_ref[chunk(c), :], NT) for qr, c in zip(qrs, cs)]
        outs = []
        for qr, s, c in zip(qrs, ss, cs):
            qcat = jnp.concatenate([(qr * dw_ref[1]).astype(BF16), (qr * dw_ref[3]).astype(BF16)], axis=1)
            outs.append(_dot((s * dw_ref[0]).astype(BF16), v_ref[chunk(c), :]) + _dot(qcat, r_ref[c]))
        for o, c in zip(outs, cs):
            g = g_ref[chunk(c), :].astype(F32)
            o_ref[chunk(c), :] = (g * _sigmoid(g) * _head_layernorm(o)).astype(o_ref.dtype)
        return carry

    lax.fori_loop(0, nc // RET_UNROLL, out_body, 0)


def _retention(proj, lg, cos2, sin2, batch, seq):
    m = proj.shape[0]
    nc = seq // CHUNK
    qb, kb = OFF_RQ // RET_DK, OFF_RK // RET_DK
    vb, gb = OFF_RV // RET_DV, OFF_RG // RET_DV
    return pl.pallas_call(
        _ret_kernel,
        grid=(batch, RET_HEADS),
        in_specs=[
            pl.BlockSpec(memory_space=pltpu.SMEM),
            pl.BlockSpec((seq, RET_DK), lambda b, h: (b, qb + h)),
            pl.BlockSpec((seq, RET_DK), lambda b, h: (b, kb + h)),
            pl.BlockSpec((seq, RET_DV), lambda b, h: (b, vb + h)),
            pl.BlockSpec((seq, RET_DV), lambda b, h: (b, gb + h)),
            pl.BlockSpec((seq, RET_DK), lambda b, h: (0, 0)),
            pl.BlockSpec((seq, RET_DK), lambda b, h: (0, 0)),
        ],
        out_specs=pl.BlockSpec((seq, RET_DV), lambda b, h: (b, h)),
        out_shape=jax.ShapeDtypeStruct((m, RET_V), BF16),
        scratch_shapes=[
            pltpu.VMEM((seq, RET_DK), BF16),
            pltpu.VMEM((nc, 2 * RET_DK, RET_DV), F32),
            pltpu.VMEM((nc, 2 * RET_DK, RET_DV), BF16),
            pltpu.VMEM((RET_DK, RET_DV), F32),
            pltpu.VMEM((RET_DK, RET_DV), F32),
            pltpu.VMEM((5, CHUNK, CHUNK), F32),
        ],
        compiler_params=_cparams(("parallel", "arbitrary")),
        name="retention",
    )(lg, proj, proj, proj, proj, cos2, sin2)


def _na_kernel(q_ref, k_ref, v_ref, bias_ref, o_ref, *s_refs):
    rows = q_ref.shape[0] // GRID_W
    band = NA_WIN_ROWS * GRID_W
    win = (NA_WIN_ROWS + 1) * GRID_W
    scale = NA_DH ** -0.5 * LOG2E
    head_a = lax.broadcasted_iota(jnp.int32, (GRID_W, LANES), 1) < NA_DH
    ones = jnp.ones((band, LANES), BF16)

    def grid_row(r, n):
        return pl.ds(pl.multiple_of(r * GRID_W, GRID_W), n)

    def band_start(r):
        return jnp.clip(r - NA_WIN_ROWS // 2, 0, rows - NA_WIN_ROWS)

    n_pairs = rows // 2

    def window_start(pair):
        return jnp.minimum(band_start(2 * pair), rows - NA_WIN_ROWS - 1)

    def scores(pair, s_ref):
        blocks = []
        for j in range(2):
            q2 = (q_ref[grid_row(2 * pair + j, GRID_W), :].astype(F32) * scale).astype(BF16)
            zero = jnp.zeros_like(q2)
            blocks += [jnp.where(head_a, q2, zero), jnp.where(head_a, zero, q2)]
        qbd = jnp.concatenate(blocks, axis=0)
        s_ref[...] = _dotg(k_ref[grid_row(window_start(pair), win), :], qbd, NT)

    def attend(pair, s_ref):
        ws = window_start(pair)
        for j in range(2):
            r = 2 * pair + j
            rs = band_start(r)
            s = s_ref[grid_row(rs - ws, band), j * LANES:(j + 1) * LANES]
            s = s + bias_ref[grid_row(rs - r + NA_WIN_ROWS - 1, band), :]
            p = jnp.exp2(s - jnp.max(s, axis=0, keepdims=True))
            v_aug = jnp.concatenate([v_ref[grid_row(rs, band), :], ones], axis=1)
            o2 = _dotg(p.astype(BF16), v_aug, TN)
            top, bot = o2[0:GRID_W, :], o2[GRID_W:2 * GRID_W, :]
            num = jnp.where(head_a, top[:, 0:LANES], bot[:, 0:LANES])
            den = jnp.where(head_a, top[:, LANES:2 * LANES], bot[:, LANES:2 * LANES])
            o_ref[grid_row(r, GRID_W), :] = (num * (1.0 / den)).astype(o_ref.dtype)

    scores(0, s_refs[0])

    def body(t, carry):
        for u in range(NA_UNROLL):
            pair = t * NA_UNROLL + u
            scores(jnp.minimum(pair + 1, n_pairs - 1), s_refs[(u + 1) % NA_UNROLL])
            attend(pair, s_refs[u])
        return carry

    lax.fori_loop(0, n_pairs // NA_UNROLL, body, 0)


NA_REL_ROWS = 2 * NA_WIN_ROWS - 1
NA_REL_COLS = 2 * NA_WIN_COLS - 1


def _na_bias_kernel(r_ref, o_ref):
    kc = lax.broadcasted_iota(jnp.int32, (GRID_W, LANES), 0)
    lane = lax.broadcasted_iota(jnp.int32, (GRID_W, LANES), 1)
    head_a = lane < GRID_W
    col = jnp.where(head_a, lane, lane - GRID_W)
    cs = jnp.clip(col - NA_WIN_COLS // 2, 0, GRID_W - NA_WIN_COLS)
    in_window = jnp.abs(2 * (kc - cs) - (NA_WIN_COLS - 1)) <= NA_WIN_COLS - 1
    for dr in range(NA_REL_ROWS):
        tiles = []
        for hh in range(LANES // GRID_W):
            row = jnp.broadcast_to(r_ref[hh, dr:dr + 1, :], (GRID_W, LANES))
            shift = (hh * GRID_W - (NA_WIN_COLS - 1)) % LANES
            tiles.append(pltpu.roll(row, shift, 1, stride=1, stride_axis=0))
        o_ref[dr * GRID_W:(dr + 1) * GRID_W, :] = jnp.where(
            in_window, jnp.where(head_a, tiles[0], tiles[1]) * LOG2E, -jnp.inf)


def _na_bias_tables(rpb):
    depth = rpb.shape[0]
    hp = LANES // NA_DH
    rows_pad = -(-NA_REL_ROWS // 8) * 8
    rev = jnp.pad(rpb.astype(F32)[..., ::-1],
                  ((0, 0), (0, 0), (0, rows_pad - NA_REL_ROWS), (0, LANES - NA_REL_COLS)))
    return pl.pallas_call(
        _na_bias_kernel,
        grid=(depth, NA_HEADS // hp),
        in_specs=[pl.BlockSpec((None, hp, rows_pad, LANES), lambda l, p: (l, p, 0, 0))],
        out_specs=pl.BlockSpec((None, None, NA_REL_ROWS * GRID_W, LANES), lambda l, p: (l, p, 0, 0)),
        out_shape=jax.ShapeDtypeStruct((depth, NA_HEADS // hp, NA_REL_ROWS * GRID_W, LANES), F32),
        compiler_params=_cparams(("parallel", "parallel")),
        name="na_bias",
    )(rev)


def _neighbourhood(proj, bias, layer, batch, seq):
    m = proj.shape[0]
    hp = LANES // NA_DH
    qb, kb, vb = OFF_NQ // LANES, OFF_NK // LANES, OFF_NV // LANES
    return pl.pallas_call(
        _na_kernel,
        grid=(NA_HEADS // hp, batch),
        in_specs=[
            pl.BlockSpec((seq, LANES), lambda p, b: (b, qb + p)),
            pl.BlockSpec((seq, LANES), lambda p, b: (b, kb + p)),
            pl.BlockSpec((seq, LANES), lambda p, b: (b, vb + p)),
            pl.BlockSpec((None, None, (2 * NA_WIN_ROWS - 1) * GRID_W, LANES), lambda p, b: (layer, p, 0, 0)),
        ],
        out_specs=pl.BlockSpec((seq, LANES), lambda p, b: (b, p)),
        out_shape=jax.ShapeDtypeStruct((m, NA_W), BF16),
        scratch_shapes=[pltpu.VMEM(((NA_WIN_ROWS + 1) * GRID_W, 2 * LANES), F32)] * NA_UNROLL,
        compiler_params=_cparams(("parallel", "arbitrary")),
        name="neighbourhood",
    )(proj, proj, proj, bias)


def _log_sigmoid(x):
    return jnp.minimum(x, 0.0) - jnp.log1p(jnp.exp(-jnp.abs(x)))


def _split3(x):
    hi = x.astype(BF16)
    r1 = x - hi.astype(F32)
    mid = r1.astype(BF16)
    lo = (r1 - mid.astype(F32)).astype(BF16)
    return hi, mid, lo


def _sublane_cummax(x, reverse):
    n = x.shape[0]
    shift = 1
    while shift < n:
        fill = jnp.full((shift, x.shape[1]), -jnp.inf, x.dtype)
        moved = (jnp.concatenate([x[shift:, :], fill], axis=0) if reverse
                 else jnp.concatenate([fill, x[:n - shift, :]], axis=0))
        x = jnp.maximum(x, moved)
        shift *= 2
    return x


def _gate_kernel(pre_ref, b_ref, row_ref):
    n_sub = pre_ref.shape[0] // CHUNK
    pi = lax.broadcasted_iota(jnp.int32, (CHUNK, CHUNK), 0)
    pj = lax.broadcasted_iota(jnp.int32, (CHUNK, CHUNK), 1)
    tri = jnp.where(pj <= pi, 1.0, 0.0).astype(BF16)
    lane = lax.broadcasted_iota(jnp.int32, (CHUNK, LANES), 1)
    kind = (lane // ML_HEADS) % 2
    is_fwd_f = (lane < N_GATE) & (kind == 1) & (lane < 2 * ML_HEADS)
    is_bwd_f = (lane < N_GATE) & (kind == 1) & (lane >= 2 * ML_HEADS)
    for s in range(n_sub):
        sl = slice(s * CHUNK, (s + 1) * CHUNK)
        x = pre_ref[sl, :] + b_ref[...]
        ls = _log_sigmoid(x)
        hi, mid, lo = _split3(ls)
        cum = _dot(tri, hi) + _dot(tri, mid) + _dot(tri, lo)
        rev = cum[CHUNK - 1:CHUNK, :] - cum + ls
        out = jnp.where(is_fwd_f, cum, jnp.where(is_bwd_f, rev, x))
        cum_on_i = pltpu.roll(out, LANES - ML_HEADS, 1)
        key_w = out - cum_on_i
        seen = jnp.where(lane < 2 * ML_HEADS, _sublane_cummax(key_w, False), _sublane_cummax(key_w, True))
        row_ref[0, 0:N_GATE, sl] = out.T[:N_GATE, :]
        row_ref[0, N_GATE:2 * N_GATE, sl] = (cum_on_i + seen).T[:N_GATE, :]


def _gates(pre, bias, batch, seq, sub=8):
    m = pre.shape[0]
    rows = sub * CHUNK
    per_b = seq // rows
    return pl.pallas_call(
        _gate_kernel,
        grid=(batch, per_b),
        in_specs=[
            pl.BlockSpec((rows, LANES), lambda b, i: (b * per_b + i, 0)),
            pl.BlockSpec((1, LANES), lambda b, i: (0, 0)),
        ],
        out_specs=pl.BlockSpec((1, 2 * N_GATE, rows), lambda b, i: (b, 0, i)),
        out_shape=jax.ShapeDtypeStruct((batch, 2 * N_GATE, seq), F32),
        compiler_params=_cparams(("parallel", "arbitrary")),
        name="mlstm_gates",
    )(pre, bias)


ST_M = 0
ST_B = 2


def _rows_to_matrix(rows, n_rows):
    width = next(r.shape[1] for r in rows if r is not None)
    rows = list(rows) + [None] * (n_rows - len(rows))
    rows = [jnp.zeros((1, width), F32) if r is None else r for r in rows]
    return jnp.concatenate(rows, axis=0).astype(BF16)


def _split3_rows(x):
    return [t.astype(F32) for t in _split3(x)]


def _ml_kernel(q_ref, k_ref, v_ref, og_ref, grow_ref, cwq_ref, cwk_ref, cbq_ref, cbk_ref,
               ng_ref, o_ref, xq_ref, xk_ref, qc_ref, kt_ref, cl_ref, nl_ref, cp_ref, np_ref,
               sl_ref, sp_ref, cst_ref, nst_ref, vst_ref):
    h = pl.program_id(1)
    seq = q_ref.shape[0]
    nc = seq // CHUNK
    scale = ML_DK ** -0.5

    def chunk(c):
        return pl.ds(pl.multiple_of(c * CHUNK, CHUNK), CHUNK)

    zero_halo = jnp.zeros((HALO, ML_DK), F32)
    for ref in (xq_ref, xk_ref):
        ref[0:HALO, :] = zero_halo
        ref[seq + HALO:seq + 2 * HALO, :] = zero_halo

    def stage_body(c, carry):
        dst = pl.ds(pl.multiple_of(c * CHUNK + HALO, HALO), CHUNK)
        xq_ref[dst, :] = q_ref[chunk(c), :].astype(F32)
        xk_ref[dst, :] = k_ref[chunk(c), :].astype(F32)
        return carry

    lax.fori_loop(0, nc, stage_body, 0)

    def conv_silu(x_ref, w_ref, b_ref, c):
        win = x_ref[pl.ds(pl.multiple_of(c * CHUNK, CHUNK), CHUNK + 2 * HALO), :]
        y = b_ref[...] + jnp.zeros((CHUNK, ML_DK), F32)
        for j in range(ML_CONV_W):
            off = HALO - ML_CONV_W // 2 + j
            y = y + win[off:off + CHUNK, :] * w_ref[j:j + 1, :]
        return y * _sigmoid(y)

    def conv_body(t, carry):
        for u in range(2):
            c = 2 * t + u
            qc_ref[chunk(c), :] = conv_silu(xq_ref, cwq_ref, cbq_ref, c) * scale
            kt_ref[c] = conv_silu(xk_ref, cwk_ref, cbk_ref, c).T
        return carry

    lax.fori_loop(0, nc // 2, conv_body, 0)

    sub = lax.broadcasted_iota(jnp.int32, (2 * N_GATE, CHUNK), 0)
    pi = lax.broadcasted_iota(jnp.int32, (CHUNK, CHUNK), 0)
    pj = lax.broadcasted_iota(jnp.int32, (CHUNK, CHUNK), 1)
    ones_tile = jnp.ones((CHUNK, LANES), BF16)
    ones_row = jnp.ones((1, CHUNK), F32)

    gate_idx = ((h, ML_HEADS + h), (2 * ML_HEADS + h, 3 * ML_HEADS + h))
    last_lane = (CHUNK - 1, 0)
    masks = (pj <= pi, pj >= pi)

    def row_vectors(gr, d):
        ir = jnp.sum(jnp.where(sub == gate_idx[d][0], gr, 0.0), axis=0, keepdims=True)
        br = jnp.sum(jnp.where(sub == gate_idx[d][1], gr, 0.0), axis=0, keepdims=True)
        return ir, br

    def lanes(x):
        return jnp.broadcast_to(x, (1, LANES))

    def local_body(t, carry):
        for u in range(ML_UNROLL):
            c = t * ML_UNROLL + u
            kt = kt_ref[c]
            gr = grow_ref[0, :, chunk(c)]
            parts = []
            for d in range(2):
                ir, br = row_vectors(gr, d)
                b_last = br[:, last_lane[d]:last_lane[d] + 1]
                a = b_last - br + ir
                m_loc = jnp.max(a, axis=1, keepdims=True)
                parts.append((kt * jnp.exp(a - m_loc)).astype(BF16))
                sl_ref[c, ST_M + d:ST_M + d + 1, :] = lanes(m_loc)
                sl_ref[c, ST_B + d:ST_B + d + 1, :] = lanes(b_last)
            v_ones = jnp.concatenate([v_ref[chunk(c), :], ones_tile], axis=1)
            res = _dot(jnp.concatenate(parts, axis=0), v_ones)
            cl_ref[c] = res[:, 0:ML_DV]
            nl_ref[c] = res[:, ML_DV:ML_DV + LANES]
        return carry

    lax.fori_loop(0, nc // ML_UNROLL, local_body, 0)

    cst_ref[...] = jnp.zeros_like(cst_ref)
    nst_ref[...] = jnp.zeros_like(nst_ref)
    vst_ref[...] = jnp.full(vst_ref.shape, -jnp.inf, F32)

    def scan_body(t, carry):
        for d, c in ((0, t), (1, nc - 1 - t)):
            rows = slice(d * ML_DK, (d + 1) * ML_DK)
            loc = sl_ref[c]
            c_old = cst_ref[d]
            n_old = nst_ref[d]
            m_old = vst_ref[ST_M + d:ST_M + d + 1, :]
            cp_ref[c, rows, :] = c_old.astype(BF16)
            np_ref[c, rows, :] = n_old.astype(BF16)
            sp_ref[c, ST_M + d:ST_M + d + 1, :] = m_old
            m_loc = loc[ST_M + d:ST_M + d + 1, :]
            b_last = loc[ST_B + d:ST_B + d + 1, :]
            m_new = jnp.maximum(b_last + m_old, m_loc)
            s_old = jnp.exp(b_last + m_old - m_new)[:, 0:1]
            s_new = jnp.exp(m_loc - m_new)[:, 0:1]
            cst_ref[d] = s_old * c_old + s_new * cl_ref[c, rows, :]
            nst_ref[d] = s_old * n_old + s_new * nl_ref[c, rows, :]
            vst_ref[ST_M + d:ST_M + d + 1, :] = m_new
        return carry

    lax.fori_loop(0, nc, scan_body, 0)

    r16 = lax.broadcasted_iota(jnp.int32, (16, 4 * LANES), 0)
    l16 = lax.broadcasted_iota(jnp.int32, (16, 4 * LANES), 1)
    col_block = 2 * (r16 >> 3) + jnp.where((r16 & 7) >= 3, 1, 0)
    col_rhs = jnp.where((l16 >> 7) == col_block, 1.0, 0.0).astype(BF16)
    s_i = lax.broadcasted_iota(jnp.int32, (2 * LANES, 2 * LANES), 0)
    s_j = lax.broadcasted_iota(jnp.int32, (2 * LANES, 2 * LANES), 1)
    sum_rhs = jnp.where((s_i >> 7) == (s_j >> 7), 1.0, 0.0).astype(BF16)

    def out_body(t, carry):
        cs = [t * ML_UNROLL + u for u in range(ML_UNROLL)]
        qs = [qc_ref[chunk(c), :] for c in cs]
        stage1 = []
        for q, c in zip(qs, cs):
            qb = q.astype(BF16)
            qk = _dot(qb, kt_ref[c].astype(BF16))
            n_prev = np_ref[c]
            qn = _dot(qb, jnp.concatenate([n_prev[0:ML_DK, :], n_prev[ML_DK:2 * ML_DK, :]], axis=1))
            gr = grow_ref[0, :, chunk(c)]
            prev = sp_ref[c]
            a_arg, b_arg, a_col = [], [], []
            for d in range(2):
                ir, br = row_vectors(gr, d)
                g = br + prev[ST_M + d:ST_M + d + 1, :]
                key_w = ir - br
                m_intra = jnp.sum(jnp.where(sub == N_GATE + gate_idx[d][0], gr, 0.0), axis=0, keepdims=True)
                m_t = jnp.maximum(g, m_intra)
                zero_pad = [None] * (8 - 6)
                a_arg += _split3_rows(br - m_t) + [ones_row] * 3 + zero_pad
                b_rows = _rows_to_matrix([ones_row] * 3 + _split3_rows(key_w), 8)
                zeros = jnp.zeros_like(b_rows)
                b_arg.append(jnp.concatenate([b_rows, zeros] if d == 0 else [zeros, b_rows], axis=1))
                a_col += _split3_rows(jnp.exp(g - m_t)) + _split3_rows(m_t) + zero_pad
            arg = _dotg(_rows_to_matrix(a_arg, 16), jnp.concatenate(b_arg, axis=0), TN)
            cols = _dotg(_rows_to_matrix(a_col, 16), col_rhs, TN)
            stage1.append((qk, qn, arg, cols))
        stage2 = []
        for qk, qn, arg, cols in stage1:
            ss = [qk * jnp.exp(jnp.where(masks[d], arg[:, d * LANES:(d + 1) * LANES], -jnp.inf))
                  for d in range(2)]
            sums = _dot(jnp.concatenate(ss, axis=1).astype(BF16), sum_rhs)
            stage2.append((ss, sums))
        outs = []
        for q, c, (qk, qn, arg, cols), (ss, sums) in zip(qs, cs, stage1, stage2):
            p_sum = None
            q_parts = []
            for d in range(2):
                s_inter = cols[:, 2 * d * LANES:(2 * d + 1) * LANES]
                m_t = cols[:, (2 * d + 1) * LANES:(2 * d + 2) * LANES]
                den = s_inter * qn[:, d * LANES:(d + 1) * LANES] + sums[:, d * LANES:(d + 1) * LANES]
                inv = 1.0 / jnp.maximum(jnp.abs(den), jnp.exp(-m_t))
                p_sum = ss[d] * inv if p_sum is None else p_sum + ss[d] * inv
                q_parts.append((q * (s_inter * inv)).astype(BF16))
            outs.append(_dot(p_sum.astype(BF16), v_ref[chunk(c), :])
                        + _dot(jnp.concatenate(q_parts, axis=1), cp_ref[c]))
        for out, c in zip(outs, cs):
            y = _head_layernorm(out) * ng_ref[...]
            o_ref[chunk(c), :] = (_sigmoid(og_ref[chunk(c), :].astype(F32)) * y).astype(o_ref.dtype)
        return carry

    lax.fori_loop(0, nc // ML_UNROLL, out_body, 0)


def _mlstm(proj, grow, conv_w, conv_b, norm_g, batch, seq):
    m = proj.shape[0]
    nc = seq // CHUNK
    qb, kb = OFF_MQ // ML_DK, OFF_MK // ML_DK
    vb, ob = OFF_MV // ML_DV, OFF_MO // ML_DV
    kw = conv_w.shape[0]
    return pl.pallas_call(
        _ml_kernel,
        grid=(batch, ML_HEADS),
        in_specs=[
            pl.BlockSpec((seq, ML_DK), lambda b, h: (b, qb + h)),
            pl.BlockSpec((seq, ML_DK), lambda b, h: (b, kb + h)),
            pl.BlockSpec((seq, ML_DV), lambda b, h: (b, vb + h)),
            pl.BlockSpec((seq, ML_DV), lambda b, h: (b, ob + h)),
            pl.BlockSpec((1, 2 * N_GATE, seq), lambda b, h: (b, 0, 0)),
            pl.BlockSpec((kw, ML_DK), lambda b, h: (0, h)),
            pl.BlockSpec((kw, ML_DK), lambda b, h: (0, ML_HEADS + h)),
            pl.BlockSpec((1, ML_DK), lambda b, h: (0, h)),
            pl.BlockSpec((1, ML_DK), lambda b, h: (0, ML_HEADS + h)),
            pl.BlockSpec((1, ML_DV), lambda b, h: (0, h)),
        ],
        out_specs=pl.BlockSpec((seq, ML_DV), lambda b, h: (b, h)),
        out_shape=jax.ShapeDtypeStruct((m, ML_V), BF16),
        scratch_shapes=[
            pltpu.VMEM((seq + 2 * HALO, ML_DK), F32),
            pltpu.VMEM((seq + 2 * HALO, ML_DK), F32),
            pltpu.VMEM((seq, ML_DK), F32),
            pltpu.VMEM((nc, ML_DK, CHUNK), F32),
            pltpu.VMEM((nc, 2 * ML_DK, ML_DV), F32),
            pltpu.VMEM((nc, 2 * ML_DK, LANES), F32),
            pltpu.VMEM((nc, 2 * ML_DK, ML_DV), BF16),
            pltpu.VMEM((nc, 2 * ML_DK, LANES), BF16),
            pltpu.VMEM((nc, 8, LANES), F32),
            pltpu.VMEM((nc, 8, LANES), F32),
            pltpu.VMEM((2, ML_DK, ML_DV), F32),
            pltpu.VMEM((2, ML_DK, LANES), F32),
            pltpu.VMEM((8, LANES), F32),
        ],
        compiler_params=_cparams(("parallel", "arbitrary")),
        name="mlstm",
    )(proj, proj, proj, proj, grow, conv_w, conv_w, conv_b, conv_b, norm_g)


def _merge_kernel(x_ref, yr_ref, yn_ref, ym_ref, mix_ref, gb_ref, wr_ref, wn_ref, wm_ref, wo_ref,
                  o_ref):
    merged = None
    for i, (y_ref, w_ref) in enumerate(((yr_ref, wr_ref), (yn_ref, wn_ref), (ym_ref, wm_ref))):
        cols = slice(i * D_MODEL, (i + 1) * D_MODEL)
        gate = _sigmoid(mix_ref[:, cols].astype(F32) + gb_ref[:, cols])
        term = gate * _dot(y_ref[...], w_ref[...])
        merged = term if merged is None else merged + term
    o_ref[...] = x_ref[...] + _dot(merged.astype(BF16), wo_ref[...])


def _merge(x2, y_ret, y_na, y_ml, proj, gate_b, w_ret_o, w_na_o, w_ml_o, w_out, layer, tm=512):
    m = x2.shape[0]
    row = lambda i: (i, 0)
    const = lambda i: (0, 0)
    wspec = pl.BlockSpec((None, D_MODEL, D_MODEL), lambda i: (layer, 0, 0))
    return pl.pallas_call(
        _merge_kernel,
        grid=(m // tm,),
        in_specs=[
            pl.BlockSpec((tm, D_MODEL), row),
            pl.BlockSpec((tm, RET_V), row),
            pl.BlockSpec((tm, NA_W), row),
            pl.BlockSpec((tm, ML_V), row),
            pl.BlockSpec((tm, N_BRANCH * D_MODEL), lambda i: (i, OFF_MIX // (N_BRANCH * D_MODEL))),
            pl.BlockSpec((1, N_BRANCH * D_MODEL), const),
            wspec, wspec, wspec, wspec,
        ],
        out_specs=pl.BlockSpec((tm, D_MODEL), row),
        out_shape=jax.ShapeDtypeStruct((m, D_MODEL), F32),
        compiler_params=_cparams(("parallel",)),
        name="merge_out",
    )(x2, y_ret, y_na, y_ml, proj, gate_b, w_ret_o, w_na_o, w_ml_o, w_out)


def _rms(x, g):
    return x * lax.rsqrt(jnp.mean(x * x, axis=-1, keepdims=True) + EPS) * g


def _ffn_kernel(x_ref, g_ref, wu_ref, wd_ref, fg_ref, o_ref, *, final_norm):
    x = x_ref[...]
    h = _rms(x, g_ref[...]).astype(BF16)
    acc = x
    for c in range(D_FF // FF_CHUNK):
        cols = slice(c * FF_CHUNK, (c + 1) * FF_CHUNK)
        a = _dot(h, wu_ref[:, cols])
        u = _dot(h, wu_ref[:, D_FF + c * FF_CHUNK:D_FF + (c + 1) * FF_CHUNK])
        act = (a * _sigmoid(a) * u).astype(BF16)
        acc = acc + _dot(act, wd_ref[cols, :])
    o_ref[...] = _rms(acc, fg_ref[...]) if final_norm else acc


def _ffn(x2, g, w_up, w_down, final_g, layer, final_norm, tm=512):
    m = x2.shape[0]
    const = lambda i: (0, 0)
    return pl.pallas_call(
        functools.partial(_ffn_kernel, final_norm=final_norm),
        grid=(m // tm,),
        in_specs=[
            pl.BlockSpec((tm, D_MODEL), lambda i: (i, 0)),
            pl.BlockSpec((1, D_MODEL), const),
            pl.BlockSpec((None, D_MODEL, 2 * D_FF), lambda i: (layer, 0, 0)),
            pl.BlockSpec((None, D_FF, D_MODEL), lambda i: (layer, 0, 0)),
            pl.BlockSpec((1, D_MODEL), const),
        ],
        out_specs=pl.BlockSpec((tm, D_MODEL), lambda i: (i, 0)),
        out_shape=jax.ShapeDtypeStruct((m, D_MODEL), F32),
        compiler_params=_cparams(("parallel",)),
        name="ffn",
    )(x2, g, w_up, w_down, final_g)


def kernel(x, norm1_g, w_in, gate_b, ret_decay_logit, na_rpb, ml_conv_w, ml_conv_b, ml_gate_b, ml_norm_g,
           w_ret_o, w_na_o, w_ml_o, w_out, norm2_g, w_ffn_up, w_ffn_down, final_g):
    batch, seq, _ = x.shape
    depth = w_in.shape[0]
    assert seq % (8 * CHUNK) == 0 and seq // GRID_W >= NA_WIN_ROWS

    half = RET_DK // 2
    inv_freq = ROPE_BASE ** (-jnp.arange(half, dtype=F32) / half)
    ang = jnp.arange(seq, dtype=F32)[:, None] * inv_freq[None, :]
    cos, sin = jnp.cos(ang), jnp.sin(ang)
    cos2 = jnp.concatenate([cos, cos], axis=-1)
    sin2 = jnp.concatenate([-sin, sin], axis=-1)

    g0, g1 = W_IN_GATE_OFF, W_IN_GATE_OFF + N_GATE
    w_main = jnp.concatenate([w_in[:, :, :g0], w_in[:, :, g1:]], axis=-1).astype(BF16)
    w_gate = jnp.pad(w_in[:, :, g0:g1], ((0, 0), (0, 0), (0, LANES - N_GATE))).astype(BF16)
    ml_bias = jnp.pad(ml_gate_b.reshape(depth, 1, N_GATE).astype(F32), ((0, 0), (0, 0), (0, LANES - N_GATE)))
    conv_w = jnp.pad(ml_conv_w.astype(F32), ((0, 0), (0, 8 - ML_CONV_W), (0, 0)))
    lg = jax.nn.log_sigmoid(ret_decay_logit.astype(F32))
    w_ret_o, w_na_o, w_ml_o, w_out = (w.astype(BF16) for w in (w_ret_o, w_na_o, w_ml_o, w_out))
    w_ffn_up, w_ffn_down = w_ffn_up.astype(BF16), w_ffn_down.astype(BF16)

    na_bias = _na_bias_tables(na_rpb)

    x2 = x.reshape(batch * seq, D_MODEL).astype(F32)
    for l in range(depth):
        proj, pre = _in_proj(x2, norm1_g[l][None].astype(F32), w_main, w_gate, l)
        y_ret = _retention(proj, lg[l], cos2, sin2, batch, seq)
        y_na = _neighbourhood(proj, na_bias, l, batch, seq)
        grow = _gates(pre, ml_bias[l], batch, seq)
        y_ml = _mlstm(proj, grow, conv_w[l], ml_conv_b[l][None].astype(F32),
                      ml_norm_g[l][None].astype(F32), batch, seq)
        x2 = _merge(x2, y_ret, y_na, y_ml, proj, gate_b[l][None].astype(F32),
                    w_ret_o, w_na_o, w_ml_o, w_out, l)
        x2 = _ffn(x2, norm2_g[l][None].astype(F32), w_ffn_up, w_ffn_down,
                  final_g[None].astype(F32), l, final_norm=(l == depth - 1))
    return x2.reshape(batch, seq, D_MODEL).astype(x.dtype)
```

```python
import functools

import jax
import jax.numpy as jnp
import numpy as np
from jax import lax
from jax.experimental import pallas as pl
from jax.experimental.pallas import tpu as pltpu

F32 = jnp.float32
BF16 = jnp.bfloat16

D_MODEL = 1024
GRID_W = 64
RET_HEADS, RET_DK, RET_DV = 4, 128, 256
NA_HEADS, NA_DH, NA_WIN_ROWS, NA_WIN_COLS = 16, 64, 8, 16
ML_HEADS, ML_DK, ML_DV, ML_CONV_W = 4, 128, 256, 5
CHUNK = 128
D_FF = 2816
ROPE_BASE = 10000.0
LOG2E = 1.4426950408889634
EPS = 1e-6
N_BRANCH = 3

RET_QK = RET_HEADS * RET_DK
RET_V = RET_HEADS * RET_DV
NA_W = NA_HEADS * NA_DH
ML_QK = ML_HEADS * ML_DK
ML_V = ML_HEADS * ML_DV
N_GATE = 4 * ML_HEADS
OFF_RQ = 0
OFF_RK = OFF_RQ + RET_QK
OFF_RV = OFF_RK + RET_QK
OFF_RG = OFF_RV + RET_V
OFF_NQ = OFF_RG + RET_V
OFF_NK = OFF_NQ + NA_W
OFF_NV = OFF_NK + NA_W
OFF_MQ = OFF_NV + NA_W
OFF_MK = OFF_MQ + ML_QK
OFF_MV = OFF_MK + ML_QK
OFF_MO = OFF_MV + ML_V
OFF_MIX = OFF_MO + ML_V
PROJ_W = OFF_MIX + N_BRANCH * D_MODEL
W_IN_GATE_OFF = OFF_MIX
LANES = 128
HALO = 8
FF_CHUNK = 256
RET_UNROLL = 4
ML_UNROLL = 4
NA_UNROLL = 8
VMEM_LIMIT = 56 * 1024 * 1024

NT = (((1,), (1,)), ((), ()))
TN = (((0,), (0,)), ((), ()))


def _dot(a, b):
    return jnp.dot(a, b, preferred_element_type=F32)


def _dotg(a, b, dims):
    return lax.dot_general(a, b, dims, preferred_element_type=F32)


def _sigmoid(x):
    return 1.0 / (1.0 + jnp.exp(-x))


def _cparams(sem):
    return pltpu.CompilerParams(dimension_semantics=sem, vmem_limit_bytes=VMEM_LIMIT)


def _in_proj_kernel(x0_ref, xn_ref, g_ref, w_ref, wg_ref, o_ref, og_ref, ha_ref, hb_ref, *, n_col):
    i = pl.program_id(0)
    j = pl.program_id(1)
    rows = x0_ref.shape[0] // n_col

    def normed(x):
        ms = jnp.mean(x * x, axis=-1, keepdims=True)
        return (x * lax.rsqrt(ms + EPS) * g_ref[...]).astype(BF16)

    @pl.when((i == 0) & (j == 0))
    def _():
        ha_ref[...] = normed(x0_ref[...])

    def step(cur_ref, nxt_ref):
        sl = pl.ds(pl.multiple_of(j * rows, rows), rows)
        nxt_ref[sl, :] = normed(xn_ref[sl, :])

        @pl.when(j == 0)
        def _():
            og_ref[...] = _dot(cur_ref[...], wg_ref[...])

        o_ref[...] = _dot(cur_ref[...], w_ref[...]).astype(o_ref.dtype)

    @pl.when(i % 2 == 0)
    def _():
        step(ha_ref, hb_ref)

    @pl.when(i % 2 == 1)
    def _():
        step(hb_ref, ha_ref)


def _in_proj(x2, g, w, wg, layer, tm=1024, tn=3072):
    m = x2.shape[0]
    n_row, n_col = m // tm, PROJ_W // tn
    return pl.pallas_call(
        functools.partial(_in_proj_kernel, n_col=n_col),
        grid=(n_row, n_col),
        in_specs=[
            pl.BlockSpec((tm, D_MODEL), lambda i, j: (0, 0)),
            pl.BlockSpec((tm, D_MODEL), lambda i, j: (jnp.minimum(i + 1, n_row - 1), 0)),
            pl.BlockSpec((1, D_MODEL), lambda i, j: (0, 0)),
            pl.BlockSpec((None, D_MODEL, tn), lambda i, j: (layer, 0, j)),
            pl.BlockSpec((None, D_MODEL, LANES), lambda i, j: (layer, 0, 0)),
        ],
        out_specs=[
            pl.BlockSpec((tm, tn), lambda i, j: (i, j)),
            pl.BlockSpec((tm, LANES), lambda i, j: (i, 0)),
        ],
        out_shape=[
            jax.ShapeDtypeStruct((m, PROJ_W), BF16),
            jax.ShapeDtypeStruct((m, LANES), F32),
        ],
        scratch_shapes=[pltpu.VMEM((tm, D_MODEL), BF16), pltpu.VMEM((tm, D_MODEL), BF16)],
        compiler_params=_cparams(("arbitrary", "arbitrary")),
        name="in_proj",
    )(x2, x2, g, w, wg)


def _lane_sum(x):
    return _dot(x.astype(BF16), jnp.ones((x.shape[1], LANES), BF16))


def _head_layernorm(o):
    mu = jnp.mean(o, axis=-1, keepdims=True)
    oc = o - mu
    return oc * lax.rsqrt(jnp.mean(oc * oc, axis=-1, keepdims=True) + EPS)


def _ret_kernel(lg_ref, q_ref, k_ref, v_ref, g_ref, cos_ref, sin_ref, o_ref,
                kr_ref, s_ref, r_ref, rf_ref, rb_ref, dw_ref):
    h = pl.program_id(1)
    nc = q_ref.shape[0] // CHUNK
    lgf = lg_ref[0, h]
    lgb = lg_ref[1, h]
    pi = lax.broadcasted_iota(jnp.int32, (CHUNK, CHUNK), 0).astype(F32)
    pj = lax.broadcasted_iota(jnp.int32, (CHUNK, CHUNK), 1).astype(F32)
    diff = pi - pj
    dw_ref[0] = jnp.where(diff >= 0, jnp.exp(lgf * jnp.maximum(diff, 0.0)),
                          jnp.exp(lgb * jnp.maximum(-diff, 0.0)))
    dw_ref[1] = jnp.exp(lgf * (pi + 1.0))
    dw_ref[2] = jnp.exp(lgf * (CHUNK - 1.0 - pi))
    dw_ref[3] = jnp.exp(lgb * (CHUNK - pi))
    dw_ref[4] = jnp.exp(lgb * pi)
    cdf = jnp.exp(jnp.full((1, RET_DV), lgf * CHUNK, F32))
    cdb = jnp.exp(jnp.full((1, RET_DV), lgb * CHUNK, F32))
    scale = RET_DK ** -0.5

    def chunk(c):
        return pl.ds(pl.multiple_of(c * CHUNK, CHUNK), CHUNK)

    def rotary(x_ref, sl):
        x = x_ref[sl, :].astype(F32)
        return x * cos_ref[sl, :] + pltpu.roll(x, RET_DK // 2, 1) * sin_ref[sl, :]

    def state_body(t, carry):
        for u in range(RET_UNROLL):
            c = t * RET_UNROLL + u
            sl = chunk(c)
            kr = rotary(k_ref, sl)
            kr_ref[sl, :] = kr.astype(BF16)
            kcat = jnp.concatenate([(kr * dw_ref[2]).astype(BF16), (kr * dw_ref[4]).astype(BF16)], axis=1)
            s_ref[c] = _dotg(kcat, v_ref[sl, :], TN)
        return carry

    lax.fori_loop(0, nc // RET_UNROLL, state_body, 0)

    rf_ref[...] = jnp.zeros_like(rf_ref)
    rb_ref[...] = jnp.zeros_like(rb_ref)

    def scan_body(t, carry):
        cb = nc - 1 - t
        rf = rf_ref[...]
        rb = rb_ref[...]
        r_ref[t, 0:RET_DK, :] = rf.astype(BF16)
        r_ref[cb, RET_DK:2 * RET_DK, :] = rb.astype(BF16)
        rf_ref[...] = rf * cdf + s_ref[t, 0:RET_DK, :]
        rb_ref[...] = rb * cdb + s_ref[cb, RET_DK:2 * RET_DK, :]
        return carry

    lax.fori_loop(0, nc, scan_body, 0)

    def out_body(t, carry):
        cs = [t * RET_UNROLL + u for u in range(RET_UNROLL)]
        qrs = [rotary(q_ref, chunk(c)) * scale for c in cs]
        ss = [_dotg(qr.astype(BF16), kr_ref[chunk(c), :], NT) for qr, c in zip(qrs, cs)]
        outs = []
        for qr, s, c in zip(qrs, ss, cs):
            qcat = jnp.concatenate([(qr * dw_ref[1]).astype(BF16), (qr * dw_ref[3]).astype(BF16)], axis=1)
            outs.append(_dot((s * dw_ref[0]).astype(BF16), v_ref[chunk(c), :]) + _dot(qcat, r_ref[c]))
        for o, c in zip(outs, cs):
            g = g_ref[chunk(c), :].astype(F32)
            o_ref[chunk(c), :] = (g * _sigmoid(g) * _head_layernorm(o)).astype(o_ref.dtype)
        return carry

    lax.fori_loop(0, nc // RET_UNROLL, out_body, 0)


def _retention(proj, lg, cos2, sin2, batch, seq):
    m = proj.shape[0]
    nc = seq // CHUNK
    qb, kb = OFF_RQ // RET_DK, OFF_RK // RET_DK
    vb, gb = OFF_RV // RET_DV, OFF_RG // RET_DV
    return pl.pallas_call(
        _ret_kernel,
        grid=(batch, RET_HEADS),
        in_specs=[
            pl.BlockSpec(memory_space=pltpu.SMEM),
            pl.BlockSpec((seq, RET_DK), lambda b, h: (b, qb + h)),
            pl.BlockSpec((seq, RET_DK), lambda b, h: (b, kb + h)),
            pl.BlockSpec((seq, RET_DV), lambda b, h: (b, vb + h)),
            pl.BlockSpec((seq, RET_DV), lambda b, h: (b, gb + h)),
            pl.BlockSpec((seq, RET_DK), lambda b, h: (0, 0)),
            pl.BlockSpec((seq, RET_DK), lambda b, h: (0, 0)),
        ],
        out_specs=pl.BlockSpec((seq, RET_DV), lambda b, h: (b, h)),
        out_shape=jax.ShapeDtypeStruct((m, RET_V), BF16),
        scratch_shapes=[
            pltpu.VMEM((seq, RET_DK), BF16),
            pltpu.VMEM((nc, 2 * RET_DK, RET_DV), F32),
            pltpu.VMEM((nc, 2 * RET_DK, RET_DV), BF16),
            pltpu.VMEM((RET_DK, RET_DV), F32),
            pltpu.VMEM((RET_DK, RET_DV), F32),
            pltpu.VMEM((5, CHUNK, CHUNK), F32),
        ],
        compiler_params=_cparams(("parallel", "arbitrary")),
        name="retention",
    )(lg, proj, proj, proj, proj, cos2, sin2)


def _na_kernel(q_ref, k_ref, v_ref, bias_ref, o_ref, *s_refs):
    rows = q_ref.shape[0] // GRID_W
    band = NA_WIN_ROWS * GRID_W
    win = (NA_WIN_ROWS + 1) * GRID_W
    scale = NA_DH ** -0.5 * LOG2E
    head_a = lax.broadcasted_iota(jnp.int32, (GRID_W, LANES), 1) < NA_DH
    ones = jnp.ones((band, LANES), BF16)

    def grid_row(r, n):
        return pl.ds(pl.multiple_of(r * GRID_W, GRID_W), n)

    def band_start(r):
        return jnp.clip(r - NA_WIN_ROWS // 2, 0, rows - NA_WIN_ROWS)

    n_pairs = rows // 2

    def window_start(pair):
        return jnp.minimum(band_start(2 * pair), rows - NA_WIN_ROWS - 1)

    def scores(pair, s_ref):
        blocks = []
        for j in range(2):
            q2 = (q_ref[grid_row(2 * pair + j, GRID_W), :].astype(F32) * scale).astype(BF16)
            zero = jnp.zeros_like(q2)
            blocks += [jnp.where(head_a, q2, zero), jnp.where(head_a, zero, q2)]
        qbd = jnp.concatenate(blocks, axis=0)
        s_ref[...] = _dotg(k_ref[grid_row(window_start(pair), win), :], qbd, NT)

    def attend(pair, s_ref):
        ws = window_start(pair)
        for j in range(2):
            r = 2 * pair + j
            rs = band_start(r)
            s = s_ref[grid_row(rs - ws, band), j * LANES:(j + 1) * LANES]
            s = s + bias_ref[grid_row(rs - r + NA_WIN_ROWS - 1, band), :]
            p = jnp.exp2(s - jnp.max(s, axis=0, keepdims=True))
            v_aug = jnp.concatenate([v_ref[grid_row(rs, band), :], ones], axis=1)
            o2 = _dotg(p.astype(BF16), v_aug, TN)
            top, bot = o2[0:GRID_W, :], o2[GRID_W:2 * GRID_W, :]
            num = jnp.where(head_a, top[:, 0:LANES], bot[:, 0:LANES])
            den = jnp.where(head_a, top[:, LANES:2 * LANES], bot[:, LANES:2 * LANES])
            o_ref[grid_row(r, GRID_W), :] = (num * (1.0 / den)).astype(o_ref.dtype)

    scores(0, s_refs[0])

    def body(t, carry):
        for u in range(NA_UNROLL):
            pair = t * NA_UNROLL + u
            scores(jnp.minimum(pair + 1, n_pairs - 1), s_refs[(u + 1) % NA_UNROLL])
            attend(pair, s_refs[u])
        return carry

    lax.fori_loop(0, n_pairs // NA_UNROLL, body, 0)


NA_REL_ROWS = 2 * NA_WIN_ROWS - 1
NA_REL_COLS = 2 * NA_WIN_COLS - 1


def _na_bias_kernel(r_ref, o_ref):
    kc = lax.broadcasted_iota(jnp.int32, (GRID_W, LANES), 0)
    lane = lax.broadcasted_iota(jnp.int32, (GRID_W, LANES), 1)
    head_a = lane < GRID_W
    col = jnp.where(head_a, lane, lane - GRID_W)
    cs = jnp.clip(col - NA_WIN_COLS // 2, 0, GRID_W - NA_WIN_COLS)
    in_window = jnp.abs(2 * (kc - cs) - (NA_WIN_COLS - 1)) <= NA_WIN_COLS - 1
    for dr in range(NA_REL_ROWS):
        tiles = []
        for hh in range(LANES // GRID_W):
            row = jnp.broadcast_to(r_ref[hh, dr:dr + 1, :], (GRID_W, LANES))
            shift = (hh * GRID_W - (NA_WIN_COLS - 1)) % LANES
            tiles.append(pltpu.roll(row, shift, 1, stride=1, stride_axis=0))
        o_ref[dr * GRID_W:(dr + 1) * GRID_W, :] = jnp.where(
            in_window, jnp.where(head_a, tiles[0], tiles[1]) * LOG2E, -jnp.inf)


def _na_bias_tables(rpb):
    depth = rpb.shape[0]
    hp = LANES // NA_DH
    rows_pad = -(-NA_REL_ROWS // 8) * 8
    rev = jnp.pad(rpb.astype(F32)[..., ::-1],
                  ((0, 0), (0, 0), (0, rows_pad - NA_REL_ROWS), (0, LANES - NA_REL_COLS)))
    return pl.pallas_call(
        _na_bias_kernel,
        grid=(depth, NA_HEADS // hp),
        in_specs=[pl.BlockSpec((None, hp, rows_pad, LANES), lambda l, p: (l, p, 0, 0))],
        out_specs=pl.BlockSpec((None, None, NA_REL_ROWS * GRID_W, LANES), lambda l, p: (l, p, 0, 0)),
        out_shape=jax.ShapeDtypeStruct((depth, NA_HEADS // hp, NA_REL_ROWS * GRID_W, LANES), F32),
        compiler_params=_cparams(("parallel", "parallel")),
        name="na_bias",
    )(rev)


def _neighbourhood(proj, bias, layer, batch, seq):
    m = proj.shape[0]
    hp = LANES // NA_DH
    qb, kb, vb = OFF_NQ // LANES, OFF_NK // LANES, OFF_NV // LANES
    return pl.pallas_call(
        _na_kernel,
        grid=(NA_HEADS // hp, batch),
        in_specs=[
            pl.BlockSpec((seq, LANES), lambda p, b: (b, qb + p)),
            pl.BlockSpec((seq, LANES), lambda p, b: (b, kb + p)),
            pl.BlockSpec((seq, LANES), lambda p, b: (b, vb + p)),
            pl.BlockSpec((None, None, (2 * NA_WIN_ROWS - 1) * GRID_W, LANES), lambda p, b: (layer, p, 0, 0)),
        ],
        out_specs=pl.BlockSpec((seq, LANES), lambda p, b: (b, p)),
        out_shape=jax.ShapeDtypeStruct((m, NA_W), BF16),
        scratch_shapes=[pltpu.VMEM(((NA_WIN_ROWS + 1) * GRID_W, 2 * LANES), F32)] * NA_UNROLL,
        compiler_params=_cparams(("parallel", "arbitrary")),
        name="neighbourhood",
    )(proj, proj, proj, bias)


def _log_sigmoid(x):
    return jnp.minimum(x, 0.0) - jnp.log1p(jnp.exp(-jnp.abs(x)))


def _split3(x):
    hi = x.astype(BF16)
    r1 = x - hi.astype(F32)
    mid = r1.astype(BF16)
    lo = (r1 - mid.astype(F32)).astype(BF16)
    return hi, mid, lo


def _sublane_cummax(x, reverse):
    n = x.shape[0]
    shift = 1
    while shift < n:
        fill = jnp.full((shift, x.shape[1]), -jnp.inf, x.dtype)
        moved = (jnp.concatenate([x[shift:, :], fill], axis=0) if reverse
                 else jnp.concatenate([fill, x[:n - shift, :]], axis=0))
        x = jnp.maximum(x, moved)
        shift *= 2
    return x


def _gate_kernel(pre_ref, b_ref, row_ref):
    n_sub = pre_ref.shape[0] // CHUNK
    pi = lax.broadcasted_iota(jnp.int32, (CHUNK, CHUNK), 0)
    pj = lax.broadcasted_iota(jnp.int32, (CHUNK, CHUNK), 1)
    tri = jnp.where(pj <= pi, 1.0, 0.0).astype(BF16)
    lane = lax.broadcasted_iota(jnp.int32, (CHUNK, LANES), 1)
    kind = (lane // ML_HEADS) % 2
    is_fwd_f = (lane < N_GATE) & (kind == 1) & (lane < 2 * ML_HEADS)
    is_bwd_f = (lane < N_GATE) & (kind == 1) & (lane >= 2 * ML_HEADS)
    for s in range(n_sub):
        sl = slice(s * CHUNK, (s + 1) * CHUNK)
        x = pre_ref[sl, :] + b_ref[...]
        ls = _log_sigmoid(x)
        hi, mid, lo = _split3(ls)
        cum = _dot(tri, hi) + _dot(tri, mid) + _dot(tri, lo)
        rev = cum[CHUNK - 1:CHUNK, :] - cum + ls
        out = jnp.where(is_fwd_f, cum, jnp.where(is_bwd_f, rev, x))
        cum_on_i = pltpu.roll(out, LANES - ML_HEADS, 1)
        key_w = out - cum_on_i
        seen = jnp.where(lane < 2 * ML_HEADS, _sublane_cummax(key_w, False), _sublane_cummax(key_w, True))
        row_ref[0, 0:N_GATE, sl] = out.T[:N_GATE, :]
        row_ref[0, N_GATE:2 * N_GATE, sl] = (cum_on_i + seen).T[:N_GATE, :]


def _gates(pre, bias, batch, seq, sub=8):
    m = pre.shape[0]
    rows = sub * CHUNK
    per_b = seq // rows
    return pl.pallas_call(
        _gate_kernel,
        grid=(batch, per_b),
        in_specs=[
            pl.BlockSpec((rows, LANES), lambda b, i: (b * per_b + i, 0)),
            pl.BlockSpec((1, LANES), lambda b, i: (0, 0)),
        ],
        out_specs=pl.BlockSpec((1, 2 * N_GATE, rows), lambda b, i: (b, 0, i)),
        out_shape=jax.ShapeDtypeStruct((batch, 2 * N_GATE, seq), F32),
        compiler_params=_cparams(("parallel", "arbitrary")),
        name="mlstm_gates",
    )(pre, bias)


ST_M = 0
ST_B = 2


def _rows_to_matrix(rows, n_rows):
    width = next(r.shape[1] for r in rows if r is not None)
    rows = list(rows) + [None] * (n_rows - len(rows))
    rows = [jnp.zeros((1, width), F32) if r is None else r for r in rows]
    return jnp.concatenate(rows, axis=0).astype(BF16)


def _split3_rows(x):
    return [t.astype(F32) for t in _split3(x)]


def _ml_kernel(q_ref, k_ref, v_ref, og_ref, grow_ref, cwq_ref, cwk_ref, cbq_ref, cbk_ref,
               ng_ref, o_ref, xq_ref, xk_ref, qc_ref, kt_ref, cl_ref, nl_ref, cp_ref, np_ref,
               sl_ref, sp_ref, cst_ref, nst_ref, vst_ref):
    h = pl.program_id(1)
    seq = q_ref.shape[0]
    nc = seq // CHUNK
    scale = ML_DK ** -0.5

    def chunk(c):
        return pl.ds(pl.multiple_of(c * CHUNK, CHUNK), CHUNK)

    zero_halo = jnp.zeros((HALO, ML_DK), F32)
    for ref in (xq_ref, xk_ref):
        ref[0:HALO, :] = zero_halo
        ref[seq + HALO:seq + 2 * HALO, :] = zero_halo

    def stage_body(c, carry):
        dst = pl.ds(pl.multiple_of(c * CHUNK + HALO, HALO), CHUNK)
        xq_ref[dst, :] = q_ref[chunk(c), :].astype(F32)
        xk_ref[dst, :] = k_ref[chunk(c), :].astype(F32)
        return carry

    lax.fori_loop(0, nc, stage_body, 0)

    def conv_silu(x_ref, w_ref, b_ref, c):
        win = x_ref[pl.ds(pl.multiple_of(c * CHUNK, CHUNK), CHUNK + 2 * HALO), :]
        y = b_ref[...] + jnp.zeros((CHUNK, ML_DK), F32)
        for j in range(ML_CONV_W):
            off = HALO - ML_CONV_W // 2 + j
            y = y + win[off:off + CHUNK, :] * w_ref[j:j + 1, :]
        return y * _sigmoid(y)

    def conv_body(t, carry):
        for u in range(2):
            c = 2 * t + u
            qc_ref[chunk(c), :] = conv_silu(xq_ref, cwq_ref, cbq_ref, c) * scale
            kt_ref[c] = conv_silu(xk_ref, cwk_ref, cbk_ref, c).T
        return carry

    lax.fori_loop(0, nc // 2, conv_body, 0)

    sub = lax.broadcasted_iota(jnp.int32, (2 * N_GATE, CHUNK), 0)
    pi = lax.broadcasted_iota(jnp.int32, (CHUNK, CHUNK), 0)
    pj = lax.broadcasted_iota(jnp.int32, (CHUNK, CHUNK), 1)
    ones_tile = jnp.ones((CHUNK, LANES), BF16)
    ones_row = jnp.ones((1, CHUNK), F32)

    gate_idx = ((h, ML_HEADS + h), (2 * ML_HEADS + h, 3 * ML_HEADS + h))
    last_lane = (CHUNK - 1, 0)
    masks = (pj <= pi, pj >= pi)

    def row_vectors(gr, d):
        ir = jnp.sum(jnp.where(sub == gate_idx[d][0], gr, 0.0), axis=0, keepdims=True)
        br = jnp.sum(jnp.where(sub == gate_idx[d][1], gr, 0.0), axis=0, keepdims=True)
        return ir, br

    def lanes(x):
        return jnp.broadcast_to(x, (1, LANES))

    def local_body(t, carry):
        for u in range(ML_UNROLL):
            c = t * ML_UNROLL + u
            kt = kt_ref[c]
            gr = grow_ref[0, :, chunk(c)]
            parts = []
            for d in range(2):
                ir, br = row_vectors(gr, d)
                b_last = br[:, last_lane[d]:last_lane[d] + 1]
                a = b_last - br + ir
                m_loc = jnp.max(a, axis=1, keepdims=True)
                parts.append((kt * jnp.exp(a - m_loc)).astype(BF16))
                sl_ref[c, ST_M + d:ST_M + d + 1, :] = lanes(m_loc)
                sl_ref[c, ST_B + d:ST_B + d + 1, :] = lanes(b_last)
            v_ones = jnp.concatenate([v_ref[chunk(c), :], ones_tile], axis=1)
            res = _dot(jnp.concatenate(parts, axis=0), v_ones)
            cl_ref[c] = res[:, 0:ML_DV]
            nl_ref[c] = res[:, ML_DV:ML_DV + LANES]
        return carry

    lax.fori_loop(0, nc // ML_UNROLL, local_body, 0)

    cst_ref[...] = jnp.zeros_like(cst_ref)
    nst_ref[...] = jnp.zeros_like(nst_ref)
    vst_ref[...] = jnp.full(vst_ref.shape, -jnp.inf, F32)

    def scan_body(t, carry):
        for d, c in ((0, t), (1, nc - 1 - t)):
            rows = slice(d * ML_DK, (d + 1) * ML_DK)
            loc = sl_ref[c]
            c_old = cst_ref[d]
            n_old = nst_ref[d]
            m_old = vst_ref[ST_M + d:ST_M + d + 1, :]
            cp_ref[c, rows, :] = c_old.astype(BF16)
            np_ref[c, rows, :] = n_old.astype(BF16)
            sp_ref[c, ST_M + d:ST_M + d + 1, :] = m_old
            m_loc = loc[ST_M + d:ST_M + d + 1, :]
            b_last = loc[ST_B + d:ST_B + d + 1, :]
            m_new = jnp.maximum(b_last + m_old, m_loc)
            s_old = jnp.exp(b_last + m_old - m_new)[:, 0:1]
            s_new = jnp.exp(m_loc - m_new)[:, 0:1]
            cst_ref[d] = s_old * c_old + s_new * cl_ref[c, rows, :]
            nst_ref[d] = s_old * n_old + s_new * nl_ref[c, rows, :]
            vst_ref[ST_M + d:ST_M + d + 1, :] = m_new
        return carry

    lax.fori_loop(0, nc, scan_body, 0)

    r16 = lax.broadcasted_iota(jnp.int32, (16, 4 * LANES), 0)
    l16 = lax.broadcasted_iota(jnp.int32, (16, 4 * LANES), 1)
    col_block = 2 * (r16 >> 3) + jnp.where((r16 & 7) >= 3, 1, 0)
    col_rhs = jnp.where((l16 >> 7) == col_block, 1.0, 0.0).astype(BF16)
    s_i = lax.broadcasted_iota(jnp.int32, (2 * LANES, 2 * LANES), 0)
    s_j = lax.broadcasted_iota(jnp.int32, (2 * LANES, 2 * LANES), 1)
    sum_rhs = jnp.where((s_i >> 7) == (s_j >> 7), 1.0, 0.0).astype(BF16)

    def out_body(t, carry):
        cs = [t * ML_UNROLL + u for u in range(ML_UNROLL)]
        qs = [qc_ref[chunk(c), :] for c in cs]
        stage1 = []
        for q, c in zip(qs, cs):
            qb = q.astype(BF16)
            qk = _dot(qb, kt_ref[c].astype(BF16))
            n_prev = np_ref[c]
            qn = _dot(qb, jnp.concatenate([n_prev[0:ML_DK, :], n_prev[ML_DK:2 * ML_DK, :]], axis=1))
            gr = grow_ref[0, :, chunk(c)]
            prev = sp_ref[c]
            a_arg, b_arg, a_col = [], [], []
            for d in range(2):
                ir, br = row_vectors(gr, d)
                g = br + prev[ST_M + d:ST_M + d + 1, :]
                key_w = ir - br
                m_intra = jnp.sum(jnp.where(sub == N_GATE + gate_idx[d][0], gr, 0.0), axis=0, keepdims=True)
                m_t = jnp.maximum(g, m_intra)
                zero_pad = [None] * (8 - 6)
                a_arg += _split3_rows(br - m_t) + [ones_row] * 3 + zero_pad
                b_rows = _rows_to_matrix([ones_row] * 3 + _split3_rows(key_w), 8)
                zeros = jnp.zeros_like(b_rows)
                b_arg.append(jnp.concatenate([b_rows, zeros] if d == 0 else [zeros, b_rows], axis=1))
                a_col += _split3_rows(jnp.exp(g - m_t)) + _split3_rows(m_t) + zero_pad
            arg = _dotg(_rows_to_matrix(a_arg, 16), jnp.concatenate(b_arg, axis=0), TN)
            cols = _dotg(_rows_to_matrix(a_col, 16), col_rhs, TN)
            stage1.append((qk, qn, arg, cols))
        stage2 = []
        for qk, qn, arg, cols in stage1:
            ss = [qk * jnp.exp(jnp.where(masks[d], arg[:, d * LANES:(d + 1) * LANES], -jnp.inf))
                  for d in range(2)]
            sums = _dot(jnp.concatenate(ss, axis=1).astype(BF16), sum_rhs)
            stage2.append((ss, sums))
        outs = []
        for q, c, (qk, qn, arg, cols), (ss, sums) in zip(qs, cs, stage1, stage2):
            p_sum = None
            q_parts = []
            for d in range(2):
                s_inter = cols[:, 2 * d * LANES:(2 * d + 1) * LANES]
                m_t = cols[:, (2 * d + 1) * LANES:(2 * d + 2) * LANES]
                den = s_inter * qn[:, d * LANES:(d + 1) * LANES] + sums[:, d * LANES:(d + 1) * LANES]
                inv = 1.0 / jnp.maximum(jnp.abs(den), jnp.exp(-m_t))
                p_sum = ss[d] * inv if p_sum is None else p_sum + ss[d] * inv
                q_parts.append((q * (s_inter * inv)).astype(BF16))
            outs.append(_dot(p_sum.astype(BF16), v_ref[chunk(c), :])
                        + _dot(jnp.concatenate(q_parts, axis=1), cp_ref[c]))
        for out, c in zip(outs, cs):
            y = _head_layernorm(out) * ng_ref[...]
            o_ref[chunk(c), :] = (_sigmoid(og_ref[chunk(c), :].astype(F32)) * y).astype(o_ref.dtype)
        return carry

    lax.fori_loop(0, nc // ML_UNROLL, out_body, 0)


def _mlstm(proj, grow, conv_w, conv_b, norm_g, batch, seq):
    m = proj.shape[0]
    nc = seq // CHUNK
    qb, kb = OFF_MQ // ML_DK, OFF_MK // ML_DK
    vb, ob = OFF_MV // ML_DV, OFF_MO // ML_DV
    kw = conv_w.shape[0]
    return pl.pallas_call(
        _ml_kernel,
        grid=(batch, ML_HEADS),
        in_specs=[
            pl.BlockSpec((seq, ML_DK), lambda b, h: (b, qb + h)),
            pl.BlockSpec((seq, ML_DK), lambda b, h: (b, kb + h)),
            pl.BlockSpec((seq, ML_DV), lambda b, h: (b, vb + h)),
            pl.BlockSpec((seq, ML_DV), lambda b, h: (b, ob + h)),
            pl.BlockSpec((1, 2 * N_GATE, seq), lambda b, h: (b, 0, 0)),
            pl.BlockSpec((kw, ML_DK), lambda b, h: (0, h)),
            pl.BlockSpec((kw, ML_DK), lambda b, h: (0, ML_HEADS + h)),
            pl.BlockSpec((1, ML_DK), lambda b, h: (0, h)),
            pl.BlockSpec((1, ML_DK), lambda b, h: (0, ML_HEADS + h)),
            pl.BlockSpec((1, ML_DV), lambda b, h: (0, h)),
        ],
        out_specs=pl.BlockSpec((seq, ML_DV), lambda b, h: (b, h)),
        out_shape=jax.ShapeDtypeStruct((m, ML_V), BF16),
        scratch_shapes=[
            pltpu.VMEM((seq + 2 * HALO, ML_DK), F32),
            pltpu.VMEM((seq + 2 * HALO, ML_DK), F32),
            pltpu.VMEM((seq, ML_DK), F32),
            pltpu.VMEM((nc, ML_DK, CHUNK), F32),
            pltpu.VMEM((nc, 2 * ML_DK, ML_DV), F32),
            pltpu.VMEM((nc, 2 * ML_DK, LANES), F32),
            pltpu.VMEM((nc, 2 * ML_DK, ML_DV), BF16),
            pltpu.VMEM((nc, 2 * ML_DK, LANES), BF16),
            pltpu.VMEM((nc, 8, LANES), F32),
            pltpu.VMEM((nc, 8, LANES), F32),
            pltpu.VMEM((2, ML_DK, ML_DV), F32),
            pltpu.VMEM((2, ML_DK, LANES), F32),
            pltpu.VMEM((8, LANES), F32),
        ],
        compiler_params=_cparams(("parallel", "arbitrary")),
        name="mlstm",
    )(proj, proj, proj, proj, grow, conv_w, conv_w, conv_b, conv_b, norm_g)


def _merge_kernel(x_ref, yr_ref, yn_ref, ym_ref, mix_ref, gb_ref, wr_ref, wn_ref, wm_ref, wo_ref,
                  o_ref):
    merged = None
    for i, (y_ref, w_ref) in enumerate(((yr_ref, wr_ref), (yn_ref, wn_ref), (ym_ref, wm_ref))):
        cols = slice(i * D_MODEL, (i + 1) * D_MODEL)
        gate = _sigmoid(mix_ref[:, cols].astype(F32) + gb_ref[:, cols])
        term = gate * _dot(y_ref[...], w_ref[...])
        merged = term if merged is None else merged + term
    o_ref[...] = x_ref[...] + _dot(merged.astype(BF16), wo_ref[...])


def _merge(x2, y_ret, y_na, y_ml, proj, gate_b, w_ret_o, w_na_o, w_ml_o, w_out, layer, tm=512):
    m = x2.shape[0]
    row = lambda i: (i, 0)
    const = lambda i: (0, 0)
    wspec = pl.BlockSpec((None, D_MODEL, D_MODEL), lambda i: (layer, 0, 0))
    return pl.pallas_call(
        _merge_kernel,
        grid=(m // tm,),
        in_specs=[
            pl.BlockSpec((tm, D_MODEL), row),
            pl.BlockSpec((tm, RET_V), row),
            pl.BlockSpec((tm, NA_W), row),
            pl.BlockSpec((tm, ML_V), row),
            pl.BlockSpec((tm, N_BRANCH * D_MODEL), lambda i: (i, OFF_MIX // (N_BRANCH * D_MODEL))),
            pl.BlockSpec((1, N_BRANCH * D_MODEL), const),
            wspec, wspec, wspec, wspec,
        ],
        out_specs=pl.BlockSpec((tm, D_MODEL), row),
        out_shape=jax.ShapeDtypeStruct((m, D_MODEL), F32),
        compiler_params=_cparams(("parallel",)),
        name="merge_out",
    )(x2, y_ret, y_na, y_ml, proj, gate_b, w_ret_o, w_na_o, w_ml_o, w_out)


def _rms(x, g):
    return x * lax.rsqrt(jnp.mean(x * x, axis=-1, keepdims=True) + EPS) * g


def _ffn_kernel(x_ref, g_ref, wu_ref, wd_ref, fg_ref, o_ref, *, final_norm):
    x = x_ref[...]
    h = _rms(x, g_ref[...]).astype(BF16)
    acc = x
    for c in range(D_FF // FF_CHUNK):
        cols = slice(c * FF_CHUNK, (c + 1) * FF_CHUNK)
        a = _dot(h, wu_ref[:, cols])
        u = _dot(h, wu_ref[:, D_FF + c * FF_CHUNK:D_FF + (c + 1) * FF_CHUNK])
        act = (a * _sigmoid(a) * u).astype(BF16)
        acc = acc + _dot(act, wd_ref[cols, :])
    o_ref[...] = _rms(acc, fg_ref[...]) if final_norm else acc


def _ffn(x2, g, w_up, w_down, final_g, layer, final_norm, tm=512):
    m = x2.shape[0]
    const = lambda i: (0, 0)
    return pl.pallas_call(
        functools.partial(_ffn_kernel, final_norm=final_norm),
        grid=(m // tm,),
        in_specs=[
            pl.BlockSpec((tm, D_MODEL), lambda i: (i, 0)),
            pl.BlockSpec((1, D_MODEL), const),
            pl.BlockSpec((None, D_MODEL, 2 * D_FF), lambda i: (layer, 0, 0)),
            pl.BlockSpec((None, D_FF, D_MODEL), lambda i: (layer, 0, 0)),
            pl.BlockSpec((1, D_MODEL), const),
        ],
        out_specs=pl.BlockSpec((tm, D_MODEL), lambda i: (i, 0)),
        out_shape=jax.ShapeDtypeStruct((m, D_MODEL), F32),
        compiler_params=_cparams(("parallel",)),
        name="ffn",
    )(x2, g, w_up, w_down, final_g)


def kernel(x, norm1_g, w_in, gate_b, ret_decay_logit, na_rpb, ml_conv_w, ml_conv_b, ml_gate_b, ml_norm_g,
           w_ret_o, w_na_o, w_ml_o, w_out, norm2_g, w_ffn_up, w_ffn_down, final_g):
    batch, seq, _ = x.shape
    depth = w_in.shape[0]
    assert seq % (8 * CHUNK) == 0 and seq // GRID_W >= NA_WIN_ROWS

    half = RET_DK // 2
    inv_freq = ROPE_BASE ** (-jnp.arange(half, dtype=F32) / half)
    ang = jnp.arange(seq, dtype=F32)[:, None] * inv_freq[None, :]
    cos, sin = jnp.cos(ang), jnp.sin(ang)
    cos2 = jnp.concatenate([cos, cos], axis=-1)
    sin2 = jnp.concatenate([-sin, sin], axis=-1)

    g0, g1 = W_IN_GATE_OFF, W_IN_GATE_OFF + N_GATE
    w_main = jnp.concatenate([w_in[:, :, :g0], w_in[:, :, g1:]], axis=-1).astype(BF16)
    w_gate = jnp.pad(w_in[:, :, g0:g1], ((0, 0), (0, 0), (0, LANES - N_GATE))).astype(BF16)
    ml_bias = jnp.pad(ml_gate_b.reshape(depth, 1, N_GATE).astype(F32), ((0, 0), (0, 0), (0, LANES - N_GATE)))
    conv_w = jnp.pad(ml_conv_w.astype(F32), ((0, 0), (0, 8 - ML_CONV_W), (0, 0)))
    lg = jax.nn.log_sigmoid(ret_decay_logit.astype(F32))
    w_ret_o, w_na_o, w_ml_o, w_out = (w.astype(BF16) for w in (w_ret_o, w_na_o, w_ml_o, w_out))
    w_ffn_up, w_ffn_down = w_ffn_up.astype(BF16), w_ffn_down.astype(BF16)

    na_bias = _na_bias_tables(na_rpb)

    x2 = x.reshape(batch * seq, D_MODEL).astype(F32)
    for l in range(depth):
        proj, pre = _in_proj(x2, norm1_g[l][None].astype(F32), w_main, w_gate, l)
        y_ret = _retention(proj, lg[l], cos2, sin2, batch, seq)
        y_na = _neighbourhood(proj, na_bias, l, batch, seq)
        grow = _gates(pre, ml_bias[l], batch, seq)
        y_ml = _mlstm(proj, grow, conv_w[l], ml_conv_b[l][None].astype(F32),
                      ml_norm_g[l][None].astype(F32), batch, seq)
        x2 = _merge(x2, y_ret, y_na, y_ml, proj, gate_b[l][None].astype(F32),
                    w_ret_o, w_na_o, w_ml_o, w_out, l)
        x2 = _ffn(x2, norm2_g[l][None].astype(F32), w_ffn_up, w_ffn_down,
                  final_g[None].astype(F32), l, final_norm=(l == depth - 1))
    return x2.reshape(batch, seq, D_MODEL).astype(x.dtype)
```

```python
import functools

import jax
import jax.numpy as jnp
import numpy as np
from jax import lax
from jax.experimental import pallas as pl
from jax.experimental.pallas import tpu as pltpu

F32 = jnp.float32
BF16 = jnp.bfloat16

D_MODEL = 1024
GRID_W = 64
RET_HEADS, RET_DK, RET_DV = 4, 128, 256
NA_HEADS, NA_DH, NA_WIN_ROWS, NA_WIN_COLS = 16, 64, 8, 16
ML_HEADS, ML_DK, ML_DV, ML_CONV_W = 4, 128, 256, 5
CHUNK = 128
D_FF = 2816
ROPE_BASE = 10000.0
LOG2E = 1.4426950408889634
EPS = 1e-6
N_BRANCH = 3

RET_QK = RET_HEADS * RET_DK
RET_V = RET_HEADS * RET_DV
NA_W = NA_HEADS * NA_DH
ML_QK = ML_HEADS * ML_DK
ML_V = ML_HEADS * ML_DV
N_GATE = 4 * ML_HEADS
OFF_RQ = 0
OFF_RK = OFF_RQ + RET_QK
OFF_RV = OFF_RK + RET_QK
OFF_RG = OFF_RV + RET_V
OFF_NQ = OFF_RG + RET_V
OFF_NK = OFF_NQ + NA_W
OFF_NV = OFF_NK + NA_W
OFF_MQ = OFF_NV + NA_W
OFF_MK = OFF_MQ + ML_QK
OFF_MV = OFF_MK + ML_QK
OFF_MO = OFF_MV + ML_V
OFF_MIX = OFF_MO + ML_V
PROJ_W = OFF_MIX + N_BRANCH * D_MODEL
W_IN_GATE_OFF = OFF_MIX
LANES = 128
CONV_PAD = 16
CONV_WIN = CHUNK + 2 * CONV_PAD
FF_CHUNK = 256
RET_UNROLL = 4
ML_UNROLL = 4
NA_UNROLL = 8
VMEM_LIMIT = 56 * 1024 * 1024

NT = (((1,), (1,)), ((), ()))
TN = (((0,), (0,)), ((), ()))


def _dot(a, b):
    return jnp.dot(a, b, preferred_element_type=F32)


def _dotg(a, b, dims):
    return lax.dot_general(a, b, dims, preferred_element_type=F32)


def _sigmoid(x):
    return 1.0 / (1.0 + jnp.exp(-x))


def _cparams(sem):
    return pltpu.CompilerParams(dimension_semantics=sem, vmem_limit_bytes=VMEM_LIMIT)


def _in_proj_kernel(x0_ref, xn_ref, g_ref, w_ref, wg_ref, o_ref, og_ref, ha_ref, hb_ref, *, n_col):
    i = pl.program_id(0)
    j = pl.program_id(1)
    rows = x0_ref.shape[0] // n_col

    def normed(x):
        ms = jnp.mean(x * x, axis=-1, keepdims=True)
        return (x * lax.rsqrt(ms + EPS) * g_ref[...]).astype(BF16)

    @pl.when((i == 0) & (j == 0))
    def _():
        ha_ref[...] = normed(x0_ref[...])

    def step(cur_ref, nxt_ref):
        sl = pl.ds(pl.multiple_of(j * rows, rows), rows)
        nxt_ref[sl, :] = normed(xn_ref[sl, :])

        @pl.when(j == 0)
        def _():
            og_ref[...] = _dotg(cur_ref[...], wg_ref[...], NT)

        o_ref[...] = _dotg(cur_ref[...], w_ref[...], NT).astype(o_ref.dtype)

    @pl.when(i % 2 == 0)
    def _():
        step(ha_ref, hb_ref)

    @pl.when(i % 2 == 1)
    def _():
        step(hb_ref, ha_ref)


def _in_proj(x2, g, w_t, layer, tm=1024, tn=3072):
    m = x2.shape[0]
    n_row, n_col = m // tm, PROJ_W // tn
    assert W_IN_GATE_OFF % tn == 0 and n_col == W_IN_GATE_OFF // tn + 1
    n_lo = W_IN_GATE_OFF // tn
    in_w = PROJ_W + N_GATE
    row0 = layer * in_w
    w_t = w_t.reshape(-1, D_MODEL)
    return pl.pallas_call(
        functools.partial(_in_proj_kernel, n_col=n_col),
        grid=(n_row, n_col),
        in_specs=[
            pl.BlockSpec((tm, D_MODEL), lambda i, j: (0, 0)),
            pl.BlockSpec((tm, D_MODEL), lambda i, j: (jnp.minimum(i + 1, n_row - 1), 0)),
            pl.BlockSpec((1, D_MODEL), lambda i, j: (0, 0)),
            pl.BlockSpec((pl.Element(tn), pl.Element(D_MODEL)),
                         lambda i, j: (pl.multiple_of(row0 + j * tn + (j // n_lo) * N_GATE, N_GATE), 0)),
            pl.BlockSpec((pl.Element(LANES), pl.Element(D_MODEL)), lambda i, j: (row0 + W_IN_GATE_OFF, 0)),
        ],
        out_specs=[
            pl.BlockSpec((tm, tn), lambda i, j: (i, j)),
            pl.BlockSpec((tm, LANES), lambda i, j: (i, 0)),
        ],
        out_shape=[
            jax.ShapeDtypeStruct((m, PROJ_W), BF16),
            jax.ShapeDtypeStruct((m, LANES), F32),
        ],
        scratch_shapes=[pltpu.VMEM((tm, D_MODEL), BF16), pltpu.VMEM((tm, D_MODEL), BF16)],
        compiler_params=_cparams(("arbitrary", "arbitrary")),
        name="in_proj",
    )(x2, x2, g, w_t, w_t)


def _lane_sum(x):
    return _dot(x.astype(BF16), jnp.ones((x.shape[1], LANES), BF16))


def _head_layernorm(o):
    mu = jnp.mean(o, axis=-1, keepdims=True)
    oc = o - mu
    return oc * lax.rsqrt(jnp.mean(oc * oc, axis=-1, keepdims=True) + EPS)


def _ret_kernel(lg_ref, q_ref, k_ref, v_ref, g_ref, cos_ref, sin_ref, o_ref,
                kr_ref, s_ref, r_ref, rf_ref, rb_ref, dw_ref):
    h = pl.program_id(1)
    nc = q_ref.shape[0] // CHUNK
    lgf = lg_ref[0, h]
    lgb = lg_ref[1, h]
    pi = lax.broadcasted_iota(jnp.int32, (CHUNK, CHUNK), 0).astype(F32)
    pj = lax.broadcasted_iota(jnp.int32, (CHUNK, CHUNK), 1).astype(F32)
    diff = pi - pj
    dw_ref[0] = jnp.where(diff >= 0, jnp.exp(lgf * jnp.maximum(diff, 0.0)),
                          jnp.exp(lgb * jnp.maximum(-diff, 0.0)))
    dw_ref[1] = jnp.exp(lgf * (pi + 1.0))
    dw_ref[2] = jnp.exp(lgf * (CHUNK - 1.0 - pi))
    dw_ref[3] = jnp.exp(lgb * (CHUNK - pi))
    dw_ref[4] = jnp.exp(lgb * pi)
    cdf = jnp.exp(jnp.full((1, RET_DV), lgf * CHUNK, F32))
    cdb = jnp.exp(jnp.full((1, RET_DV), lgb * CHUNK, F32))
    scale = RET_DK ** -0.5

    def chunk(c):
        return pl.ds(pl.multiple_of(c * CHUNK, CHUNK), CHUNK)

    def rotary(x_ref, sl):
        x = x_ref[sl, :].astype(F32)
        return x * cos_ref[sl, :] + pltpu.roll(x, RET_DK // 2, 1) * sin_ref[sl, :]

    def state_body(t, carry):
        cs = [t * RET_UNROLL + u for u in range(RET_UNROLL)]
        kcats = []
        for c in cs:
            kr = rotary(k_ref, chunk(c))
            kr_ref[chunk(c), :] = kr.astype(BF16)
            kcats.append(jnp.concatenate([(kr * dw_ref[2]).astype(BF16), (kr * dw_ref[4]).astype(BF16)], axis=1))
        for c, kcat in zip(cs, kcats):
            s_ref[c] = _dotg(kcat, v_ref[chunk(c), :], TN)
        return carry

    lax.fori_loop(0, nc // RET_UNROLL, state_body, 0)

    rf_ref[...] = jnp.zeros_like(rf_ref)
    rb_ref[...] = jnp.zeros_like(rb_ref)

    def scan_body(t, carry):
        cb = nc - 1 - t
        rf = rf_ref[...]
        rb = rb_ref[...]
        r_ref[t, 0:RET_DK, :] = rf.astype(BF16)
        r_ref[cb, RET_DK:2 * RET_DK, :] = rb.astype(BF16)
        rf_ref[...] = rf * cdf + s_ref[t, 0:RET_DK, :]
        rb_ref[...] = rb * cdb + s_ref[cb, RET_DK:2 * RET_DK, :]
        return carry

    lax.fori_loop(0, nc, scan_body, 0)

    def out_body(t, carry):
        cs = [t * RET_UNROLL + u for u in range(RET_UNROLL)]
        qrs = [rotary(q_ref, chunk(c)) * scale for c in cs]
        ss = [_dotg(qr.astype(BF16), kr_ref[chunk(c), :], NT) for qr, c in zip(qrs, cs)]
        outs = []
        for qr, s, c in zip(qrs, ss, cs):
            qcat = jnp.concatenate([(qr * dw_ref[1]).astype(BF16), (qr * dw_ref[3]).astype(BF16)], axis=1)
            outs.append(_dot((s * dw_ref[0]).astype(BF16), v_ref[chunk(c), :]) + _dot(qcat, r_ref[c]))
        for o, c in zip(outs, cs):
            g = g_ref[chunk(c), :].astype(F32)
            o_ref[chunk(c), :] = (g * _sigmoid(g) * _head_layernorm(o)).astype(o_ref.dtype)
        return carry

    lax.fori_loop(0, nc // RET_UNROLL, out_body, 0)


def _retention(proj, lg, cos2, sin2, batch, seq):
    m = proj.shape[0]
    nc = seq // CHUNK
    qb, kb = OFF_RQ // RET_DK, OFF_RK // RET_DK
    vb, gb = OFF_RV // RET_DV, OFF_RG // RET_DV
    return pl.pallas_call(
        _ret_kernel,
        grid=(batch, RET_HEADS),
        in_specs=[
            pl.BlockSpec(memory_space=pltpu.SMEM),
            pl.BlockSpec((seq, RET_DK), lambda b, h: (b, qb + h)),
            pl.BlockSpec((seq, RET_DK), lambda b, h: (b, kb + h)),
            pl.BlockSpec((seq, RET_DV), lambda b, h: (b, vb + h)),
            pl.BlockSpec((seq, RET_DV), lambda b, h: (b, gb + h)),
            pl.BlockSpec((seq, RET_DK), lambda b, h: (0, 0)),
            pl.BlockSpec((seq, RET_DK), lambda b, h: (0, 0)),
        ],
        out_specs=pl.BlockSpec((seq, RET_DV), lambda b, h: (b, h)),
        out_shape=jax.ShapeDtypeStruct((m, RET_V), BF16),
        scratch_shapes=[
            pltpu.VMEM((seq, RET_DK), BF16),
            pltpu.VMEM((nc, 2 * RET_DK, RET_DV), F32),
            pltpu.VMEM((nc, 2 * RET_DK, RET_DV), BF16),
            pltpu.VMEM((RET_DK, RET_DV), F32),
            pltpu.VMEM((RET_DK, RET_DV), F32),
            pltpu.VMEM((5, CHUNK, CHUNK), F32),
        ],
        compiler_params=_cparams(("parallel", "arbitrary")),
        name="retention",
    )(lg, proj, proj, proj, proj, cos2, sin2)


def _na_kernel(q_ref, k_ref, v_ref, bias_ref, o_ref, *s_refs):
    rows = q_ref.shape[0] // GRID_W
    band = NA_WIN_ROWS * GRID_W
    win = (NA_WIN_ROWS + 1) * GRID_W
    scale = NA_DH ** -0.5 * LOG2E
    head_a = lax.broadcasted_iota(jnp.int32, (GRID_W, LANES), 1) < NA_DH
    ones = jnp.ones((band, LANES), BF16)

    def grid_row(r, n):
        return pl.ds(pl.multiple_of(r * GRID_W, GRID_W), n)

    def band_start(r):
        return jnp.clip(r - NA_WIN_ROWS // 2, 0, rows - NA_WIN_ROWS)

    n_pairs = rows // 2

    def window_start(pair):
        return jnp.minimum(band_start(2 * pair), rows - NA_WIN_ROWS - 1)

    def scores(pair, s_ref):
        blocks = []
        for j in range(2):
            q2 = (q_ref[grid_row(2 * pair + j, GRID_W), :].astype(F32) * scale).astype(BF16)
            zero = jnp.zeros_like(q2)
            blocks += [jnp.where(head_a, q2, zero), jnp.where(head_a, zero, q2)]
        qbd = jnp.concatenate(blocks, axis=0)
        s_ref[...] = _dotg(k_ref[grid_row(window_start(pair), win), :], qbd, NT)

    def attend(pair, s_ref):
        ws = window_start(pair)
        for j in range(2):
            r = 2 * pair + j
            rs = band_start(r)
            s = s_ref[grid_row(rs - ws, band), j * LANES:(j + 1) * LANES]
            s = s + bias_ref[grid_row(rs - r + NA_WIN_ROWS - 1, band), :]
            p = jnp.exp2(s - jnp.max(s, axis=0, keepdims=True))
            v_aug = jnp.concatenate([v_ref[grid_row(rs, band), :], ones], axis=1)
            o2 = _dotg(p.astype(BF16), v_aug, TN)
            top, bot = o2[0:GRID_W, :], o2[GRID_W:2 * GRID_W, :]
            num = jnp.where(head_a, top[:, 0:LANES], bot[:, 0:LANES])
            den = jnp.where(head_a, top[:, LANES:2 * LANES], bot[:, LANES:2 * LANES])
            o_ref[grid_row(r, GRID_W), :] = (num * (1.0 / den)).astype(o_ref.dtype)

    scores(0, s_refs[0])

    def body(t, carry):
        for u in range(NA_UNROLL):
            pair = t * NA_UNROLL + u
            scores(jnp.minimum(pair + 1, n_pairs - 1), s_refs[(u + 1) % NA_UNROLL])
            attend(pair, s_refs[u])
        return carry

    lax.fori_loop(0, n_pairs // NA_UNROLL, body, 0)


NA_REL_ROWS = 2 * NA_WIN_ROWS - 1
NA_REL_COLS = 2 * NA_WIN_COLS - 1


def _na_bias_kernel(r_ref, o_ref):
    kc = lax.broadcasted_iota(jnp.int32, (GRID_W, LANES), 0)
    lane = lax.broadcasted_iota(jnp.int32, (GRID_W, LANES), 1)
    head_a = lane < GRID_W
    col = jnp.where(head_a, lane, lane - GRID_W)
    cs = jnp.clip(col - NA_WIN_COLS // 2, 0, GRID_W - NA_WIN_COLS)
    in_window = jnp.abs(2 * (kc - cs) - (NA_WIN_COLS - 1)) <= NA_WIN_COLS - 1
    for dr in range(NA_REL_ROWS):
        tiles = []
        for hh in range(LANES // GRID_W):
            row = jnp.broadcast_to(r_ref[hh, dr:dr + 1, :], (GRID_W, LANES))
            shift = (hh * GRID_W - (NA_WIN_COLS - 1)) % LANES
            tiles.append(pltpu.roll(row, shift, 1, stride=1, stride_axis=0))
        o_ref[dr * GRID_W:(dr + 1) * GRID_W, :] = jnp.where(
            in_window, jnp.where(head_a, tiles[0], tiles[1]) * LOG2E, -jnp.inf)


def _na_bias_tables(rpb):
    depth = rpb.shape[0]
    hp = LANES // NA_DH
    rows_pad = -(-NA_REL_ROWS // 8) * 8
    rev = jnp.pad(rpb.astype(F32)[..., ::-1],
                  ((0, 0), (0, 0), (0, rows_pad - NA_REL_ROWS), (0, LANES - NA_REL_COLS)))
    return pl.pallas_call(
        _na_bias_kernel,
        grid=(depth, NA_HEADS // hp),
        in_specs=[pl.BlockSpec((None, hp, rows_pad, LANES), lambda l, p: (l, p, 0, 0))],
        out_specs=pl.BlockSpec((None, None, NA_REL_ROWS * GRID_W, LANES), lambda l, p: (l, p, 0, 0)),
        out_shape=jax.ShapeDtypeStruct((depth, NA_HEADS // hp, NA_REL_ROWS * GRID_W, LANES), F32),
        compiler_params=_cparams(("parallel", "parallel")),
        name="na_bias",
    )(rev)


def _neighbourhood(proj, bias, layer, batch, seq):
    m = proj.shape[0]
    hp = LANES // NA_DH
    qb, kb, vb = OFF_NQ // LANES, OFF_NK // LANES, OFF_NV // LANES
    return pl.pallas_call(
        _na_kernel,
        grid=(NA_HEADS // hp, batch),
        in_specs=[
            pl.BlockSpec((seq, LANES), lambda p, b: (b, qb + p)),
            pl.BlockSpec((seq, LANES), lambda p, b: (b, kb + p)),
            pl.BlockSpec((seq, LANES), lambda p, b: (b, vb + p)),
            pl.BlockSpec((None, None, (2 * NA_WIN_ROWS - 1) * GRID_W, LANES), lambda p, b: (layer, p, 0, 0)),
        ],
        out_specs=pl.BlockSpec((seq, LANES), lambda p, b: (b, p)),
        out_shape=jax.ShapeDtypeStruct((m, NA_W), BF16),
        scratch_shapes=[pltpu.VMEM(((NA_WIN_ROWS + 1) * GRID_W, 2 * LANES), F32)] * NA_UNROLL,
        compiler_params=_cparams(("parallel", "arbitrary")),
        name="neighbourhood",
    )(proj, proj, proj, bias)


def _log_sigmoid(x):
    return jnp.minimum(x, 0.0) - jnp.log1p(jnp.exp(-jnp.abs(x)))


def _split3(x):
    hi = x.astype(BF16)
    r1 = x - hi.astype(F32)
    mid = r1.astype(BF16)
    lo = (r1 - mid.astype(F32)).astype(BF16)
    return hi, mid, lo


def _sublane_cummax(x, reverse):
    n = x.shape[0]
    shift = 1
    while shift < n:
        fill = jnp.full((shift, x.shape[1]), -jnp.inf, x.dtype)
        moved = (jnp.concatenate([x[shift:, :], fill], axis=0) if reverse
                 else jnp.concatenate([fill, x[:n - shift, :]], axis=0))
        x = jnp.maximum(x, moved)
        shift *= 2
    return x


def _gate_kernel(pre_ref, b_ref, row_ref):
    n_sub = pre_ref.shape[0] // CHUNK
    pi = lax.broadcasted_iota(jnp.int32, (CHUNK, CHUNK), 0)
    pj = lax.broadcasted_iota(jnp.int32, (CHUNK, CHUNK), 1)
    tri = jnp.where(pj <= pi, 1.0, 0.0).astype(BF16)
    lane = lax.broadcasted_iota(jnp.int32, (CHUNK, LANES), 1)
    kind = (lane // ML_HEADS) % 2
    is_fwd_f = (lane < N_GATE) & (kind == 1) & (lane < 2 * ML_HEADS)
    is_bwd_f = (lane < N_GATE) & (kind == 1) & (lane >= 2 * ML_HEADS)
    for s in range(n_sub):
        sl = slice(s * CHUNK, (s + 1) * CHUNK)
        x = pre_ref[sl, :] + b_ref[...]
        ls = _log_sigmoid(x)
        hi, mid, lo = _split3(ls)
        cum = _dot(tri, hi) + _dot(tri, mid) + _dot(tri, lo)
        rev = cum[CHUNK - 1:CHUNK, :] - cum + ls
        out = jnp.where(is_fwd_f, cum, jnp.where(is_bwd_f, rev, x))
        cum_on_i = pltpu.roll(out, LANES - ML_HEADS, 1)
        key_w = out - cum_on_i
        seen = jnp.where(lane < 2 * ML_HEADS, _sublane_cummax(key_w, False), _sublane_cummax(key_w, True))
        row_ref[0, 0:N_GATE, sl] = out.T[:N_GATE, :]
        row_ref[0, N_GATE:2 * N_GATE, sl] = (cum_on_i + seen).T[:N_GATE, :]


def _gates(pre, bias, batch, seq, sub=8):
    m = pre.shape[0]
    rows = sub * CHUNK
    per_b = seq // rows
    return pl.pallas_call(
        _gate_kernel,
        grid=(batch, per_b),
        in_specs=[
            pl.BlockSpec((rows, LANES), lambda b, i: (b * per_b + i, 0)),
            pl.BlockSpec((1, LANES), lambda b, i: (0, 0)),
        ],
        out_specs=pl.BlockSpec((1, 2 * N_GATE, rows), lambda b, i: (b, 0, i)),
        out_shape=jax.ShapeDtypeStruct((batch, 2 * N_GATE, seq), F32),
        compiler_params=_cparams(("parallel", "arbitrary")),
        name="mlstm_gates",
    )(pre, bias)


ST_M = 0
ST_B = 2


def _rows_to_matrix(rows, n_rows):
    width = next(r.shape[1] for r in rows if r is not None)
    rows = list(rows) + [None] * (n_rows - len(rows))
    rows = [jnp.zeros((1, width), F32) if r is None else r for r in rows]
    return jnp.concatenate(rows, axis=0).astype(BF16)


def _split3_rows(x):
    return [t.astype(F32) for t in _split3(x)]


def _ml_kernel(q_ref, k_ref, v_ref, og_ref, grow_ref, cwq_ref, cwk_ref, cbq_ref, cbk_ref,
               ng_ref, o_ref, shift_ref, qc_ref, kt_ref, cl_ref, nl_ref, cp_ref, np_ref,
               sl_ref, sp_ref, cst_ref, nst_ref, vst_ref):
    h = pl.program_id(1)
    seq = q_ref.shape[0]
    nc = seq // CHUNK
    scale = ML_DK ** -0.5

    def chunk(c):
        return pl.ds(pl.multiple_of(c * CHUNK, CHUNK), CHUNK)

    taps = [j for j in range(ML_CONV_W) if j != ML_CONV_W // 2]
    s_row = lax.broadcasted_iota(jnp.int32, (len(taps) * CHUNK, CONV_WIN), 0)
    s_col = lax.broadcasted_iota(jnp.int32, (len(taps) * CHUNK, CONV_WIN), 1)
    tap_off = jnp.zeros_like(s_row)
    for i, j in enumerate(taps):
        tap_off = jnp.where(s_row >> 7 == i, j - ML_CONV_W // 2, tap_off)
    for variant, base in enumerate((0, CONV_PAD, 2 * CONV_PAD)):
        shift_ref[variant] = jnp.where(s_col == (s_row & (CHUNK - 1)) + tap_off + base, 1.0, 0.0).astype(BF16)

    def conv_body(t, carry):
        cs = [ML_UNROLL * t + u for u in range(ML_UNROLL)]
        shifteds = []
        for c in cs:
            start = pl.multiple_of(jnp.clip(c * CHUNK - CONV_PAD, 0, seq - CONV_WIN), CONV_PAD)
            variant = jnp.where(c == 0, 0, jnp.where(c == nc - 1, 2, 1))
            win = jnp.concatenate([q_ref[pl.ds(start, CONV_WIN), :], k_ref[pl.ds(start, CONV_WIN), :]], axis=1)
            shifteds.append(_dot(shift_ref[variant], win))
        for c, shifted in zip(cs, shifteds):
            outs = []
            for idx, (x_ref, w_ref, b_ref) in enumerate(((q_ref, cwq_ref, cbq_ref), (k_ref, cwk_ref, cbk_ref))):
                mid = ML_CONV_W // 2
                y = b_ref[...] + x_ref[chunk(c), :].astype(F32) * w_ref[mid:mid + 1, :]
                for i, j in enumerate(taps):
                    y = y + shifted[i * CHUNK:(i + 1) * CHUNK, idx * ML_DK:(idx + 1) * ML_DK] * w_ref[j:j + 1, :]
                outs.append(y * _sigmoid(y))
            qc_ref[chunk(c), :] = outs[0] * scale
            kt_ref[c] = outs[1].T
        return carry

    lax.fori_loop(0, nc // ML_UNROLL, conv_body, 0)

    sub = lax.broadcasted_iota(jnp.int32, (2 * N_GATE, CHUNK), 0)
    pi = lax.broadcasted_iota(jnp.int32, (CHUNK, CHUNK), 0)
    pj = lax.broadcasted_iota(jnp.int32, (CHUNK, CHUNK), 1)
    ones_tile = jnp.ones((CHUNK, LANES), BF16)
    ones_row = jnp.ones((1, CHUNK), F32)

    gate_idx = ((h, ML_HEADS + h), (2 * ML_HEADS + h, 3 * ML_HEADS + h))
    last_lane = (CHUNK - 1, 0)
    masks = (pj <= pi, pj >= pi)

    def row_vectors(gr, d):
        ir = jnp.sum(jnp.where(sub == gate_idx[d][0], gr, 0.0), axis=0, keepdims=True)
        br = jnp.sum(jnp.where(sub == gate_idx[d][1], gr, 0.0), axis=0, keepdims=True)
        return ir, br

    def lanes(x):
        return jnp.broadcast_to(x, (1, LANES))

    def local_body(t, carry):
        cs = [t * ML_UNROLL + u for u in range(ML_UNROLL)]
        weighted = []
        for c in cs:
            kt = kt_ref[c]
            gr = grow_ref[0, :, chunk(c)]
            parts = []
            for d in range(2):
                ir, br = row_vectors(gr, d)
                b_last = br[:, last_lane[d]:last_lane[d] + 1]
                a = b_last - br + ir
                m_loc = jnp.max(a, axis=1, keepdims=True)
                parts.append((kt * jnp.exp(a - m_loc)).astype(BF16))
                sl_ref[c, ST_M + d:ST_M + d + 1, :] = lanes(m_loc)
                sl_ref[c, ST_B + d:ST_B + d + 1, :] = lanes(b_last)
            weighted.append(jnp.concatenate(parts, axis=0))
        for c, kw in zip(cs, weighted):
            v_ones = jnp.concatenate([v_ref[chunk(c), :], ones_tile], axis=1)
            res = _dot(kw, v_ones)
            cl_ref[c] = res[:, 0:ML_DV]
            nl_ref[c] = res[:, ML_DV:ML_DV + LANES]
        return carry

    lax.fori_loop(0, nc // ML_UNROLL, local_body, 0)

    cst_ref[...] = jnp.zeros_like(cst_ref)
    nst_ref[...] = jnp.zeros_like(nst_ref)
    vst_ref[...] = jnp.full(vst_ref.shape, -jnp.inf, F32)

    def scan_body(t, carry):
        for d, c in ((0, t), (1, nc - 1 - t)):
            rows = slice(d * ML_DK, (d + 1) * ML_DK)
            loc = sl_ref[c]
            c_old = cst_ref[d]
            n_old = nst_ref[d]
            m_old = vst_ref[ST_M + d:ST_M + d + 1, :]
            cp_ref[c, rows, :] = c_old.astype(BF16)
            np_ref[c, rows, :] = n_old.astype(BF16)
            sp_ref[c, ST_M + d:ST_M + d + 1, :] = m_old
            m_loc = loc[ST_M + d:ST_M + d + 1, :]
            b_last = loc[ST_B + d:ST_B + d + 1, :]
            m_new = jnp.maximum(b_last + m_old, m_loc)
            s_old = jnp.exp(b_last + m_old - m_new)[:, 0:1]
            s_new = jnp.exp(m_loc - m_new)[:, 0:1]
            cst_ref[d] = s_old * c_old + s_new * cl_ref[c, rows, :]
            nst_ref[d] = s_old * n_old + s_new * nl_ref[c, rows, :]
            vst_ref[ST_M + d:ST_M + d + 1, :] = m_new
        return carry

    lax.fori_loop(0, nc, scan_body, 0)

    r16 = lax.broadcasted_iota(jnp.int32, (16, 4 * LANES), 0)
    l16 = lax.broadcasted_iota(jnp.int32, (16, 4 * LANES), 1)
    col_block = 2 * (r16 >> 3) + jnp.where((r16 & 7) >= 3, 1, 0)
    col_rhs = jnp.where((l16 >> 7) == col_block, 1.0, 0.0).astype(BF16)
    s_i = lax.broadcasted_iota(jnp.int32, (2 * LANES, 2 * LANES), 0)
    s_j = lax.broadcasted_iota(jnp.int32, (2 * LANES, 2 * LANES), 1)
    sum_rhs = jnp.where((s_i >> 7) == (s_j >> 7), 1.0, 0.0).astype(BF16)

    def out_body(t, carry):
        cs = [t * ML_UNROLL + u for u in range(ML_UNROLL)]
        qs = [qc_ref[chunk(c), :] for c in cs]
        stage1 = []
        for q, c in zip(qs, cs):
            qb = q.astype(BF16)
            qk = _dot(qb, kt_ref[c].astype(BF16))
            n_prev = np_ref[c]
            qn = _dot(qb, jnp.concatenate([n_prev[0:ML_DK, :], n_prev[ML_DK:2 * ML_DK, :]], axis=1))
            gr = grow_ref[0, :, chunk(c)]
            prev = sp_ref[c]
            a_arg, b_arg, a_col = [], [], []
            for d in range(2):
                ir, br = row_vectors(gr, d)
                g = br + prev[ST_M + d:ST_M + d + 1, :]
                key_w = ir - br
                m_intra = jnp.sum(jnp.where(sub == N_GATE + gate_idx[d][0], gr, 0.0), axis=0, keepdims=True)
                m_t = jnp.maximum(g, m_intra)
                zero_pad = [None] * (8 - 6)
                a_arg += _split3_rows(br - m_t) + [ones_row] * 3 + zero_pad
                b_rows = _rows_to_matrix([ones_row] * 3 + _split3_rows(key_w), 8)
                zeros = jnp.zeros_like(b_rows)
                b_arg.append(jnp.concatenate([b_rows, zeros] if d == 0 else [zeros, b_rows], axis=1))
                a_col += _split3_rows(jnp.exp(g - m_t)) + _split3_rows(m_t) + zero_pad
            arg = _dotg(_rows_to_matrix(a_arg, 16), jnp.concatenate(b_arg, axis=0), TN)
            cols = _dotg(_rows_to_matrix(a_col, 16), col_rhs, TN)
            stage1.append((qk, qn, arg, cols))
        stage2 = []
        for qk, qn, arg, cols in stage1:
            ss = [qk * jnp.exp(jnp.where(masks[d], arg[:, d * LANES:(d + 1) * LANES], -jnp.inf))
                  for d in range(2)]
            sums = _dot(jnp.concatenate(ss, axis=1).astype(BF16), sum_rhs)
            stage2.append((ss, sums))
        outs = []
        for q, c, (qk, qn, arg, cols), (ss, sums) in zip(qs, cs, stage1, stage2):
            p_sum = None
            q_parts = []
            for d in range(2):
                s_inter = cols[:, 2 * d * LANES:(2 * d + 1) * LANES]
                m_t = cols[:, (2 * d + 1) * LANES:(2 * d + 2) * LANES]
                den = s_inter * qn[:, d * LANES:(d + 1) * LANES] + sums[:, d * LANES:(d + 1) * LANES]
                inv = 1.0 / jnp.maximum(jnp.abs(den), jnp.exp(-m_t))
                p_sum = ss[d] * inv if p_sum is None else p_sum + ss[d] * inv
                q_parts.append((q * (s_inter * inv)).astype(BF16))
            outs.append(_dot(p_sum.astype(BF16), v_ref[chunk(c), :])
                        + _dot(jnp.concatenate(q_parts, axis=1), cp_ref[c]))
        for out, c in zip(outs, cs):
            y = _head_layernorm(out) * ng_ref[...]
            o_ref[chunk(c), :] = (_sigmoid(og_ref[chunk(c), :].astype(F32)) * y).astype(o_ref.dtype)
        return carry

    lax.fori_loop(0, nc // ML_UNROLL, out_body, 0)


def _mlstm(proj, grow, conv_w, conv_b, norm_g, batch, seq):
    m = proj.shape[0]
    nc = seq // CHUNK
    qb, kb = OFF_MQ // ML_DK, OFF_MK // ML_DK
    vb, ob = OFF_MV // ML_DV, OFF_MO // ML_DV
    kw = conv_w.shape[0]
    return pl.pallas_call(
        _ml_kernel,
        grid=(batch, ML_HEADS),
        in_specs=[
            pl.BlockSpec((seq, ML_DK), lambda b, h: (b, qb + h)),
            pl.BlockSpec((seq, ML_DK), lambda b, h: (b, kb + h)),
            pl.BlockSpec((seq, ML_DV), lambda b, h: (b, vb + h)),
            pl.BlockSpec((seq, ML_DV), lambda b, h: (b, ob + h)),
            pl.BlockSpec((1, 2 * N_GATE, seq), lambda b, h: (b, 0, 0)),
            pl.BlockSpec((kw, ML_DK), lambda b, h: (0, h)),
            pl.BlockSpec((kw, ML_DK), lambda b, h: (0, ML_HEADS + h)),
            pl.BlockSpec((1, ML_DK), lambda b, h: (0, h)),
            pl.BlockSpec((1, ML_DK), lambda b, h: (0, ML_HEADS + h)),
            pl.BlockSpec((1, ML_DV), lambda b, h: (0, h)),
        ],
        out_specs=pl.BlockSpec((seq, ML_DV), lambda b, h: (b, h)),
        out_shape=jax.ShapeDtypeStruct((m, ML_V), BF16),
        scratch_shapes=[
            pltpu.VMEM((3, (ML_CONV_W - 1) * CHUNK, CONV_WIN), BF16),
            pltpu.VMEM((seq, ML_DK), F32),
            pltpu.VMEM((nc, ML_DK, CHUNK), F32),
            pltpu.VMEM((nc, 2 * ML_DK, ML_DV), F32),
            pltpu.VMEM((nc, 2 * ML_DK, LANES), F32),
            pltpu.VMEM((nc, 2 * ML_DK, ML_DV), BF16),
            pltpu.VMEM((nc, 2 * ML_DK, LANES), BF16),
            pltpu.VMEM((nc, 8, LANES), F32),
            pltpu.VMEM((nc, 8, LANES), F32),
            pltpu.VMEM((2, ML_DK, ML_DV), F32),
            pltpu.VMEM((2, ML_DK, LANES), F32),
            pltpu.VMEM((8, LANES), F32),
        ],
        compiler_params=_cparams(("parallel", "arbitrary")),
        name="mlstm",
    )(proj, proj, proj, proj, grow, conv_w, conv_w, conv_b, conv_b, norm_g)


def _merge_kernel(x_ref, yr_ref, yn_ref, ym_ref, mix_ref, gb_ref, wr_ref, wn_ref, wm_ref, wo_ref,
                  o_ref):
    merged = None
    for i, (y_ref, w_ref) in enumerate(((yr_ref, wr_ref), (yn_ref, wn_ref), (ym_ref, wm_ref))):
        cols = slice(i * D_MODEL, (i + 1) * D_MODEL)
        gate = _sigmoid(mix_ref[:, cols].astype(F32) + gb_ref[:, cols])
        term = gate * _dot(y_ref[...], w_ref[...])
        merged = term if merged is None else merged + term
    o_ref[...] = x_ref[...] + _dot(merged.astype(BF16), wo_ref[...])


def _merge(x2, y_ret, y_na, y_ml, proj, gate_b, w_ret_o, w_na_o, w_ml_o, w_out, layer, tm=512):
    m = x2.shape[0]
    row = lambda i: (i, 0)
    const = lambda i: (0, 0)
    wspec = pl.BlockSpec((None, D_MODEL, D_MODEL), lambda i: (layer, 0, 0))
    return pl.pallas_call(
        _merge_kernel,
        grid=(m // tm,),
        in_specs=[
            pl.BlockSpec((tm, D_MODEL), row),
            pl.BlockSpec((tm, RET_V), row),
            pl.BlockSpec((tm, NA_W), row),
            pl.BlockSpec((tm, ML_V), row),
            pl.BlockSpec((tm, N_BRANCH * D_MODEL), lambda i: (i, OFF_MIX // (N_BRANCH * D_MODEL))),
            pl.BlockSpec((1, N_BRANCH * D_MODEL), const),
            wspec, wspec, wspec, wspec,
        ],
        out_specs=pl.BlockSpec((tm, D_MODEL), row),
        out_shape=jax.ShapeDtypeStruct((m, D_MODEL), F32),
        compiler_params=_cparams(("parallel",)),
        name="merge_out",
    )(x2, y_ret, y_na, y_ml, proj, gate_b, w_ret_o, w_na_o, w_ml_o, w_out)


def _rms(x, g):
    return x * lax.rsqrt(jnp.mean(x * x, axis=-1, keepdims=True) + EPS) * g


def _ffn_kernel(x_ref, g_ref, wu_ref, wd_ref, fg_ref, o_ref, *, final_norm):
    x = x_ref[...]
    h = _rms(x, g_ref[...]).astype(BF16)
    acc = x
    for c in range(D_FF // FF_CHUNK):
        cols = slice(c * FF_CHUNK, (c + 1) * FF_CHUNK)
        a = _dot(h, wu_ref[:, cols])
        u = _dot(h, wu_ref[:, D_FF + c * FF_CHUNK:D_FF + (c + 1) * FF_CHUNK])
        act = (a * _sigmoid(a) * u).astype(BF16)
        acc = acc + _dot(act, wd_ref[cols, :])
    o_ref[...] = _rms(acc, fg_ref[...]) if final_norm else acc


def _ffn(x2, g, w_up, w_down, final_g, layer, final_norm, tm=512):
    m = x2.shape[0]
    const = lambda i: (0, 0)
    return pl.pallas_call(
        functools.partial(_ffn_kernel, final_norm=final_norm),
        grid=(m // tm,),
        in_specs=[
            pl.BlockSpec((tm, D_MODEL), lambda i: (i, 0)),
            pl.BlockSpec((1, D_MODEL), const),
            pl.BlockSpec((None, D_MODEL, 2 * D_FF), lambda i: (layer, 0, 0)),
            pl.BlockSpec((None, D_FF, D_MODEL), lambda i: (layer, 0, 0)),
            pl.BlockSpec((1, D_MODEL), const),
        ],
        out_specs=pl.BlockSpec((tm, D_MODEL), lambda i: (i, 0)),
        out_shape=jax.ShapeDtypeStruct((m, D_MODEL), F32),
        compiler_params=_cparams(("parallel",)),
        name="ffn",
    )(x2, g, w_up, w_down, final_g)


def kernel(x, norm1_g, w_in, gate_b, ret_decay_logit, na_rpb, ml_conv_w, ml_conv_b, ml_gate_b, ml_norm_g,
           w_ret_o, w_na_o, w_ml_o, w_out, norm2_g, w_ffn_up, w_ffn_down, final_g):
    batch, seq, _ = x.shape
    depth = w_in.shape[0]
    assert seq % (8 * CHUNK) == 0 and seq // GRID_W >= NA_WIN_ROWS

    half = RET_DK // 2
    inv_freq = ROPE_BASE ** (-jnp.arange(half, dtype=F32) / half)
    ang = jnp.arange(seq, dtype=F32)[:, None] * inv_freq[None, :]
    cos, sin = jnp.cos(ang), jnp.sin(ang)
    cos2 = jnp.concatenate([cos, cos], axis=-1)
    sin2 = jnp.concatenate([-sin, sin], axis=-1)

    w_in_t = jnp.swapaxes(w_in, 1, 2).astype(BF16)
    ml_bias = jnp.pad(ml_gate_b.reshape(depth, 1, N_GATE).astype(F32), ((0, 0), (0, 0), (0, LANES - N_GATE)))
    conv_w = jnp.pad(ml_conv_w.astype(F32), ((0, 0), (0, 8 - ML_CONV_W), (0, 0)))
    lg = jax.nn.log_sigmoid(ret_decay_logit.astype(F32))
    w_ret_o, w_na_o, w_ml_o, w_out = (w.astype(BF16) for w in (w_ret_o, w_na_o, w_ml_o, w_out))
    w_ffn_up, w_ffn_down = w_ffn_up.astype(BF16), w_ffn_down.astype(BF16)

    na_bias = _na_bias_tables(na_rpb)

    x2 = x.reshape(batch * seq, D_MODEL).astype(F32)
    for l in range(depth):
        proj, pre = _in_proj(x2, norm1_g[l][None].astype(F32), w_in_t, l)
        y_ret = _retention(proj, lg[l], cos2, sin2, batch, seq)
        y_na = _neighbourhood(proj, na_bias, l, batch, seq)
        grow = _gates(pre, ml_bias[l], batch, seq)
        y_ml = _mlstm(proj, grow, conv_w[l], ml_conv_b[l][None].astype(F32),
                      ml_norm_g[l][None].astype(F32), batch, seq)
        x2 = _merge(x2, y_ret, y_na, y_ml, proj, gate_b[l][None].astype(F32),
                    w_ret_o, w_na_o, w_ml_o, w_out, l)
        x2 = _ffn(x2, norm2_g[l][None].astype(F32), w_ffn_up, w_ffn_down,
                  final_g[None].astype(F32), l, final_norm=(l == depth - 1))
    return x2.reshape(batch, seq, D_MODEL).astype(x.dtype)
```

```python
import functools
import math

import jax
import jax.numpy as jnp
import numpy as np
from jax import lax
from jax.experimental import pallas as pl
from jax.experimental.pallas import tpu as pltpu

F32 = jnp.float32
BF16 = jnp.bfloat16

D_MODEL = 1024
GRID_W = 64
RET_HEADS, RET_DK, RET_DV = 4, 128, 256
NA_HEADS, NA_DH, NA_WIN_ROWS, NA_WIN_COLS = 16, 64, 8, 16
ML_HEADS, ML_DK, ML_DV, ML_CONV_W = 4, 128, 256, 5
CHUNK = 128
D_FF = 2816
ROPE_BASE = 10000.0
LOG2E = 1.4426950408889634
EPS = 1e-6
N_BRANCH = 3

RET_QK = RET_HEADS * RET_DK
RET_V = RET_HEADS * RET_DV
NA_W = NA_HEADS * NA_DH
ML_QK = ML_HEADS * ML_DK
ML_V = ML_HEADS * ML_DV
N_GATE = 4 * ML_HEADS
OFF_RQ = 0
OFF_RK = OFF_RQ + RET_QK
OFF_RV = OFF_RK + RET_QK
OFF_RG = OFF_RV + RET_V
OFF_NQ = OFF_RG + RET_V
OFF_NK = OFF_NQ + NA_W
OFF_NV = OFF_NK + NA_W
OFF_MQ = OFF_NV + NA_W
OFF_MK = OFF_MQ + ML_QK
OFF_MV = OFF_MK + ML_QK
OFF_MO = OFF_MV + ML_V
OFF_MIX = OFF_MO + ML_V
PROJ_W = OFF_MIX + N_BRANCH * D_MODEL
W_IN_GATE_OFF = OFF_MIX
LANES = 128
CONV_PAD = 16
CONV_WIN = CHUNK + 2 * CONV_PAD
FF_CHUNK = 256
RET_UNROLL = 16
ML_UNROLL = 16
NA_UNROLL = 16
VMEM_LIMIT = 56 * 1024 * 1024

NT = (((1,), (1,)), ((), ()))
TN = (((0,), (0,)), ((), ()))


def _dot(a, b):
    return jnp.dot(a, b, preferred_element_type=F32)


def _dotg(a, b, dims):
    return lax.dot_general(a, b, dims, preferred_element_type=F32)


def _sigmoid(x):
    return 1.0 / (1.0 + jnp.exp(-x))


def _cparams(sem):
    return pltpu.CompilerParams(dimension_semantics=sem, vmem_limit_bytes=VMEM_LIMIT)


def _in_proj_kernel(x0_ref, xn_ref, g_ref, w_ref, wg_ref, o_ref, og_ref, ha_ref, hb_ref, *, n_col):
    i = pl.program_id(0)
    j = pl.program_id(1)
    rows = x0_ref.shape[0] // n_col

    def normed(x):
        ms = jnp.mean(x * x, axis=-1, keepdims=True)
        return (x * lax.rsqrt(ms + EPS) * g_ref[...]).astype(BF16)

    @pl.when((i == 0) & (j == 0))
    def _():
        ha_ref[...] = normed(x0_ref[...])

    def step(cur_ref, nxt_ref):
        sl = pl.ds(pl.multiple_of(j * rows, rows), rows)
        nxt_ref[sl, :] = normed(xn_ref[sl, :])

        @pl.when(j == 0)
        def _():
            og_ref[...] = _dotg(cur_ref[...], wg_ref[...], NT)

        o_ref[...] = _dotg(cur_ref[...], w_ref[...], NT).astype(o_ref.dtype)

    @pl.when(i % 2 == 0)
    def _():
        step(ha_ref, hb_ref)

    @pl.when(i % 2 == 1)
    def _():
        step(hb_ref, ha_ref)


def _in_proj(x2, g, w_t, layer, tm=1024, tn=3072):
    m = x2.shape[0]
    n_row, n_col = m // tm, PROJ_W // tn
    assert W_IN_GATE_OFF % tn == 0 and n_col == W_IN_GATE_OFF // tn + 1
    n_lo = W_IN_GATE_OFF // tn
    in_w = PROJ_W + N_GATE
    row0 = layer * in_w
    w_t = w_t.reshape(-1, D_MODEL)
    return pl.pallas_call(
        functools.partial(_in_proj_kernel, n_col=n_col),
        grid=(n_row, n_col),
        in_specs=[
            pl.BlockSpec((tm, D_MODEL), lambda i, j: (0, 0)),
            pl.BlockSpec((tm, D_MODEL), lambda i, j: (jnp.minimum(i + 1, n_row - 1), 0)),
            pl.BlockSpec((1, D_MODEL), lambda i, j: (0, 0)),
            pl.BlockSpec((pl.Element(tn), pl.Element(D_MODEL)),
                         lambda i, j: (pl.multiple_of(row0 + j * tn + (j // n_lo) * N_GATE, N_GATE), 0)),
            pl.BlockSpec((pl.Element(LANES), pl.Element(D_MODEL)), lambda i, j: (row0 + W_IN_GATE_OFF, 0)),
        ],
        out_specs=[
            pl.BlockSpec((tm, tn), lambda i, j: (i, j)),
            pl.BlockSpec((tm, LANES), lambda i, j: (i, 0)),
        ],
        out_shape=[
            jax.ShapeDtypeStruct((m, PROJ_W), BF16),
            jax.ShapeDtypeStruct((m, LANES), F32),
        ],
        scratch_shapes=[pltpu.VMEM((tm, D_MODEL), BF16), pltpu.VMEM((tm, D_MODEL), BF16)],
        compiler_params=_cparams(("arbitrary", "arbitrary")),
        name="in_proj",
    )(x2, x2, g, w_t, w_t)


def _lane_sum(x):
    return _dot(x.astype(BF16), jnp.ones((x.shape[1], LANES), BF16))


def _head_layernorm(o):
    mu = jnp.mean(o, axis=-1, keepdims=True)
    oc = o - mu
    return oc * lax.rsqrt(jnp.mean(oc * oc, axis=-1, keepdims=True) + EPS)


def _ret_kernel(lg_ref, q_ref, k_ref, v_ref, g_ref, cos_ref, sin_ref, o_ref,
                kr_ref, s_ref, r_ref, rf_ref, rb_ref, dw_ref):
    h = pl.program_id(1)
    nc = q_ref.shape[0] // CHUNK
    unroll = math.gcd(RET_UNROLL, nc)
    lgf = lg_ref[0, h]
    lgb = lg_ref[1, h]
    pi = lax.broadcasted_iota(jnp.int32, (CHUNK, CHUNK), 0).astype(F32)
    pj = lax.broadcasted_iota(jnp.int32, (CHUNK, CHUNK), 1).astype(F32)
    diff = pi - pj
    dw_ref[0] = jnp.where(diff >= 0, jnp.exp(lgf * jnp.maximum(diff, 0.0)),
                          jnp.exp(lgb * jnp.maximum(-diff, 0.0)))
    dw_ref[1] = jnp.exp(lgf * (pi + 1.0))
    dw_ref[2] = jnp.exp(lgf * (CHUNK - 1.0 - pi))
    dw_ref[3] = jnp.exp(lgb * (CHUNK - pi))
    dw_ref[4] = jnp.exp(lgb * pi)
    cdf = jnp.exp(jnp.full((1, RET_DV), lgf * CHUNK, F32))
    cdb = jnp.exp(jnp.full((1, RET_DV), lgb * CHUNK, F32))
    scale = RET_DK ** -0.5

    def chunk(c):
        return pl.ds(pl.multiple_of(c * CHUNK, CHUNK), CHUNK)

    def rotary(x_ref, sl):
        x = x_ref[sl, :].astype(F32)
        return x * cos_ref[sl, :] + pltpu.roll(x, RET_DK // 2, 1) * sin_ref[sl, :]

    def state_body(t, carry):
        cs = [t * unroll + u for u in range(unroll)]
        kcats = []
        for c in cs:
            kr = rotary(k_ref, chunk(c))
            kr_ref[chunk(c), :] = kr.astype(BF16)
            kcats.append(jnp.concatenate([(kr * dw_ref[2]).astype(BF16), (kr * dw_ref[4]).astype(BF16)], axis=1))
        for c, kcat in zip(cs, kcats):
            s_ref[c] = _dotg(kcat, v_ref[chunk(c), :], TN)
        return carry

    lax.fori_loop(0, nc // unroll, state_body, 0)

    rf_ref[...] = jnp.zeros_like(rf_ref)
    rb_ref[...] = jnp.zeros_like(rb_ref)

    def scan_body(t, carry):
        cb = nc - 1 - t
        rf = rf_ref[...]
        rb = rb_ref[...]
        r_ref[t, 0:RET_DK, :] = rf.astype(BF16)
        r_ref[cb, RET_DK:2 * RET_DK, :] = rb.astype(BF16)
        rf_ref[...] = rf * cdf + s_ref[t, 0:RET_DK, :]
        rb_ref[...] = rb * cdb + s_ref[cb, RET_DK:2 * RET_DK, :]
        return carry

    lax.fori_loop(0, nc, scan_body, 0)

    def out_body(t, carry):
        cs = [t * unroll + u for u in range(unroll)]
        qrs = [rotary(q_ref, chunk(c)) * scale for c in cs]
        ss = [_dotg(qr.astype(BF16), kr_ref[chunk(c), :], NT) for qr, c in zip(qrs, cs)]
        outs = []
        for qr, s, c in zip(qrs, ss, cs):
            qcat = jnp.concatenate([(qr * dw_ref[1]).astype(BF16), (qr * dw_ref[3]).astype(BF16)], axis=1)
            outs.append(_dot((s * dw_ref[0]).astype(BF16), v_ref[chunk(c), :]) + _dot(qcat, r_ref[c]))
        for o, c in zip(outs, cs):
            g = g_ref[chunk(c), :].astype(F32)
            o_ref[chunk(c), :] = (g * _sigmoid(g) * _head_layernorm(o)).astype(o_ref.dtype)
        return carry

    lax.fori_loop(0, nc // unroll, out_body, 0)


def _retention(proj, lg, cos2, sin2, batch, seq):
    m = proj.shape[0]
    nc = seq // CHUNK
    qb, kb = OFF_RQ // RET_DK, OFF_RK // RET_DK
    vb, gb = OFF_RV // RET_DV, OFF_RG // RET_DV
    return pl.pallas_call(
        _ret_kernel,
        grid=(batch, RET_HEADS),
        in_specs=[
            pl.BlockSpec(memory_space=pltpu.SMEM),
            pl.BlockSpec((seq, RET_DK), lambda b, h: (b, qb + h)),
            pl.BlockSpec((seq, RET_DK), lambda b, h: (b, kb + h)),
            pl.BlockSpec((seq, RET_DV), lambda b, h: (b, vb + h)),
            pl.BlockSpec((seq, RET_DV), lambda b, h: (b, gb + h)),
            pl.BlockSpec((seq, RET_DK), lambda b, h: (0, 0)),
            pl.BlockSpec((seq, RET_DK), lambda b, h: (0, 0)),
        ],
        out_specs=pl.BlockSpec((seq, RET_DV), lambda b, h: (b, h)),
        out_shape=jax.ShapeDtypeStruct((m, RET_V), BF16),
        scratch_shapes=[
            pltpu.VMEM((seq, RET_DK), BF16),
            pltpu.VMEM((nc, 2 * RET_DK, RET_DV), F32),
            pltpu.VMEM((nc, 2 * RET_DK, RET_DV), BF16),
            pltpu.VMEM((RET_DK, RET_DV), F32),
            pltpu.VMEM((RET_DK, RET_DV), F32),
            pltpu.VMEM((5, CHUNK, CHUNK), F32),
        ],
        compiler_params=_cparams(("parallel", "arbitrary")),
        name="retention",
    )(lg, proj, proj, proj, proj, cos2, sin2)


def _na_kernel(q_ref, k_ref, v_ref, bias_ref, o_ref, *s_refs):
    rows = q_ref.shape[0] // GRID_W
    band = NA_WIN_ROWS * GRID_W
    win = (NA_WIN_ROWS + 1) * GRID_W
    scale = NA_DH ** -0.5 * LOG2E
    head_a = lax.broadcasted_iota(jnp.int32, (GRID_W, LANES), 1) < NA_DH
    ones = jnp.ones((band, LANES), BF16)

    def grid_row(r, n):
        return pl.ds(pl.multiple_of(r * GRID_W, GRID_W), n)

    def band_start(r):
        return jnp.clip(r - NA_WIN_ROWS // 2, 0, rows - NA_WIN_ROWS)

    n_pairs = rows // 2
    unroll = len(s_refs)

    def window_start(pair):
        return jnp.minimum(band_start(2 * pair), rows - NA_WIN_ROWS - 1)

    def scores(pair, s_ref):
        blocks = []
        for j in range(2):
            q2 = (q_ref[grid_row(2 * pair + j, GRID_W), :].astype(F32) * scale).astype(BF16)
            zero = jnp.zeros_like(q2)
            blocks += [jnp.where(head_a, q2, zero), jnp.where(head_a, zero, q2)]
        qbd = jnp.concatenate(blocks, axis=0)
        s_ref[...] = _dotg(k_ref[grid_row(window_start(pair), win), :], qbd, NT)

    def attend(pair, s_ref):
        ws = window_start(pair)
        for j in range(2):
            r = 2 * pair + j
            rs = band_start(r)
            s = s_ref[grid_row(rs - ws, band), j * LANES:(j + 1) * LANES]
            s = s + bias_ref[grid_row(rs - r + NA_WIN_ROWS - 1, band), :]
            p = jnp.exp2(s - jnp.max(s, axis=0, keepdims=True))
            v_aug = jnp.concatenate([v_ref[grid_row(rs, band), :], ones], axis=1)
            o2 = _dotg(p.astype(BF16), v_aug, TN)
            top, bot = o2[0:GRID_W, :], o2[GRID_W:2 * GRID_W, :]
            num = jnp.where(head_a, top[:, 0:LANES], bot[:, 0:LANES])
            den = jnp.where(head_a, top[:, LANES:2 * LANES], bot[:, LANES:2 * LANES])
            o_ref[grid_row(r, GRID_W), :] = (num * (1.0 / den)).astype(o_ref.dtype)

    scores(0, s_refs[0])

    def body(t, carry):
        for u in range(unroll):
            pair = t * unroll + u
            scores(jnp.minimum(pair + 1, n_pairs - 1), s_refs[(u + 1) % unroll])
            attend(pair, s_refs[u])
        return carry

    lax.fori_loop(0, n_pairs // unroll, body, 0)


NA_REL_ROWS = 2 * NA_WIN_ROWS - 1
NA_REL_COLS = 2 * NA_WIN_COLS - 1


def _na_bias_kernel(r_ref, o_ref):
    kc = lax.broadcasted_iota(jnp.int32, (GRID_W, LANES), 0)
    lane = lax.broadcasted_iota(jnp.int32, (GRID_W, LANES), 1)
    head_a = lane < GRID_W
    col = jnp.where(head_a, lane, lane - GRID_W)
    cs = jnp.clip(col - NA_WIN_COLS // 2, 0, GRID_W - NA_WIN_COLS)
    in_window = jnp.abs(2 * (kc - cs) - (NA_WIN_COLS - 1)) <= NA_WIN_COLS - 1
    for dr in range(NA_REL_ROWS):
        tiles = []
        for hh in range(LANES // GRID_W):
            row = jnp.broadcast_to(r_ref[hh, dr:dr + 1, :], (GRID_W, LANES))
            shift = (hh * GRID_W - (NA_WIN_COLS - 1)) % LANES
            tiles.append(pltpu.roll(row, shift, 1, stride=1, stride_axis=0))
        o_ref[dr * GRID_W:(dr + 1) * GRID_W, :] = jnp.where(
            in_window, jnp.where(head_a, tiles[0], tiles[1]) * LOG2E, -jnp.inf)


def _na_bias_tables(rpb):
    depth = rpb.shape[0]
    hp = LANES // NA_DH
    rows_pad = -(-NA_REL_ROWS // 8) * 8
    rev = jnp.pad(rpb.astype(F32)[..., ::-1],
                  ((0, 0), (0, 0), (0, rows_pad - NA_REL_ROWS), (0, LANES - NA_REL_COLS)))
    return pl.pallas_call(
        _na_bias_kernel,
        grid=(depth, NA_HEADS // hp),
        in_specs=[pl.BlockSpec((None, hp, rows_pad, LANES), lambda l, p: (l, p, 0, 0))],
        out_specs=pl.BlockSpec((None, None, NA_REL_ROWS * GRID_W, LANES), lambda l, p: (l, p, 0, 0)),
        out_shape=jax.ShapeDtypeStruct((depth, NA_HEADS // hp, NA_REL_ROWS * GRID_W, LANES), F32),
        compiler_params=_cparams(("parallel", "parallel")),
        name="na_bias",
    )(rev)


def _neighbourhood(proj, bias, layer, batch, seq):
    m = proj.shape[0]
    hp = LANES // NA_DH
    qb, kb, vb = OFF_NQ // LANES, OFF_NK // LANES, OFF_NV // LANES
    return pl.pallas_call(
        _na_kernel,
        grid=(NA_HEADS // hp, batch),
        in_specs=[
            pl.BlockSpec((seq, LANES), lambda p, b: (b, qb + p)),
            pl.BlockSpec((seq, LANES), lambda p, b: (b, kb + p)),
            pl.BlockSpec((seq, LANES), lambda p, b: (b, vb + p)),
            pl.BlockSpec((None, None, (2 * NA_WIN_ROWS - 1) * GRID_W, LANES), lambda p, b: (layer, p, 0, 0)),
        ],
        out_specs=pl.BlockSpec((seq, LANES), lambda p, b: (b, p)),
        out_shape=jax.ShapeDtypeStruct((m, NA_W), BF16),
        scratch_shapes=[pltpu.VMEM(((NA_WIN_ROWS + 1) * GRID_W, 2 * LANES), F32)]
        * math.gcd(NA_UNROLL, seq // GRID_W // 2),
        compiler_params=_cparams(("parallel", "arbitrary")),
        name="neighbourhood",
    )(proj, proj, proj, bias)


def _log_sigmoid(x):
    return jnp.minimum(x, 0.0) - jnp.log1p(jnp.exp(-jnp.abs(x)))


def _split3(x):
    hi = x.astype(BF16)
    r1 = x - hi.astype(F32)
    mid = r1.astype(BF16)
    lo = (r1 - mid.astype(F32)).astype(BF16)
    return hi, mid, lo


def _sublane_cummax(x, reverse):
    n = x.shape[0]
    shift = 1
    while shift < n:
        fill = jnp.full((shift, x.shape[1]), -jnp.inf, x.dtype)
        moved = (jnp.concatenate([x[shift:, :], fill], axis=0) if reverse
                 else jnp.concatenate([fill, x[:n - shift, :]], axis=0))
        x = jnp.maximum(x, moved)
        shift *= 2
    return x


def _gate_kernel(pre_ref, b_ref, row_ref):
    n_sub = pre_ref.shape[0] // CHUNK
    pi = lax.broadcasted_iota(jnp.int32, (CHUNK, CHUNK), 0)
    pj = lax.broadcasted_iota(jnp.int32, (CHUNK, CHUNK), 1)
    tri = jnp.where(pj <= pi, 1.0, 0.0).astype(BF16)
    lane = lax.broadcasted_iota(jnp.int32, (CHUNK, LANES), 1)
    kind = (lane // ML_HEADS) % 2
    is_fwd_f = (lane < N_GATE) & (kind == 1) & (lane < 2 * ML_HEADS)
    is_bwd_f = (lane < N_GATE) & (kind == 1) & (lane >= 2 * ML_HEADS)
    for s in range(n_sub):
        sl = slice(s * CHUNK, (s + 1) * CHUNK)
        x = pre_ref[sl, :] + b_ref[...]
        ls = _log_sigmoid(x)
        hi, mid, lo = _split3(ls)
        cum = _dot(tri, hi) + _dot(tri, mid) + _dot(tri, lo)
        rev = cum[CHUNK - 1:CHUNK, :] - cum + ls
        out = jnp.where(is_fwd_f, cum, jnp.where(is_bwd_f, rev, x))
        cum_on_i = pltpu.roll(out, LANES - ML_HEADS, 1)
        key_w = out - cum_on_i
        seen = jnp.where(lane < 2 * ML_HEADS, _sublane_cummax(key_w, False), _sublane_cummax(key_w, True))
        row_ref[0, 0:N_GATE, sl] = out.T[:N_GATE, :]
        row_ref[0, N_GATE:2 * N_GATE, sl] = (cum_on_i + seen).T[:N_GATE, :]


def _gates(pre, bias, batch, seq, sub=8):
    m = pre.shape[0]
    rows = sub * CHUNK
    per_b = seq // rows
    return pl.pallas_call(
        _gate_kernel,
        grid=(batch, per_b),
        in_specs=[
            pl.BlockSpec((rows, LANES), lambda b, i: (b * per_b + i, 0)),
            pl.BlockSpec((1, LANES), lambda b, i: (0, 0)),
        ],
        out_specs=pl.BlockSpec((1, 2 * N_GATE, rows), lambda b, i: (b, 0, i)),
        out_shape=jax.ShapeDtypeStruct((batch, 2 * N_GATE, seq), F32),
        compiler_params=_cparams(("parallel", "arbitrary")),
        name="mlstm_gates",
    )(pre, bias)


ST_M = 0
ST_B = 2


def _rows_to_matrix(rows, n_rows):
    width = next(r.shape[1] for r in rows if r is not None)
    rows = list(rows) + [None] * (n_rows - len(rows))
    rows = [jnp.zeros((1, width), F32) if r is None else r for r in rows]
    return jnp.concatenate(rows, axis=0).astype(BF16)


def _split3_rows(x):
    return [t.astype(F32) for t in _split3(x)]


def _ml_kernel(q_ref, k_ref, v_ref, og_ref, grow_ref, cwq_ref, cwk_ref, cbq_ref, cbk_ref,
               ng_ref, o_ref, shift_ref, qc_ref, kt_ref, cl_ref, nl_ref, cp_ref, np_ref,
               sl_ref, sp_ref, cst_ref, nst_ref, vst_ref):
    h = pl.program_id(1)
    seq = q_ref.shape[0]
    nc = seq // CHUNK
    unroll = math.gcd(ML_UNROLL, nc)
    scale = ML_DK ** -0.5

    def chunk(c):
        return pl.ds(pl.multiple_of(c * CHUNK, CHUNK), CHUNK)

    taps = [j for j in range(ML_CONV_W) if j != ML_CONV_W // 2]
    s_row = lax.broadcasted_iota(jnp.int32, (len(taps) * CHUNK, CONV_WIN), 0)
    s_col = lax.broadcasted_iota(jnp.int32, (len(taps) * CHUNK, CONV_WIN), 1)
    tap_off = jnp.zeros_like(s_row)
    for i, j in enumerate(taps):
        tap_off = jnp.where(s_row >> 7 == i, j - ML_CONV_W // 2, tap_off)
    for variant, base in enumerate((0, CONV_PAD, 2 * CONV_PAD)):
        shift_ref[variant] = jnp.where(s_col == (s_row & (CHUNK - 1)) + tap_off + base, 1.0, 0.0).astype(BF16)

    def conv_body(t, carry):
        cs = [t * unroll + u for u in range(unroll)]
        shifteds = []
        for c in cs:
            start = pl.multiple_of(jnp.clip(c * CHUNK - CONV_PAD, 0, seq - CONV_WIN), CONV_PAD)
            variant = jnp.where(c == 0, 0, jnp.where(c == nc - 1, 2, 1))
            win = jnp.concatenate([q_ref[pl.ds(start, CONV_WIN), :], k_ref[pl.ds(start, CONV_WIN), :]], axis=1)
            shifteds.append(_dot(shift_ref[variant], win))
        for c, shifted in zip(cs, shifteds):
            outs = []
            for idx, (x_ref, w_ref, b_ref) in enumerate(((q_ref, cwq_ref, cbq_ref), (k_ref, cwk_ref, cbk_ref))):
                mid = ML_CONV_W // 2
                y = b_ref[...] + x_ref[chunk(c), :].astype(F32) * w_ref[mid:mid + 1, :]
                for i, j in enumerate(taps):
                    y = y + shifted[i * CHUNK:(i + 1) * CHUNK, idx * ML_DK:(idx + 1) * ML_DK] * w_ref[j:j + 1, :]
                outs.append(y * _sigmoid(y))
            qc_ref[chunk(c), :] = outs[0] * scale
            kt_ref[c] = outs[1].T
        return carry

    lax.fori_loop(0, nc // unroll, conv_body, 0)

    sub = lax.broadcasted_iota(jnp.int32, (2 * N_GATE, CHUNK), 0)
    pi = lax.broadcasted_iota(jnp.int32, (CHUNK, CHUNK), 0)
    pj = lax.broadcasted_iota(jnp.int32, (CHUNK, CHUNK), 1)
    ones_tile = jnp.ones((CHUNK, LANES), BF16)
    ones_row = jnp.ones((1, CHUNK), F32)

    gate_idx = ((h, ML_HEADS + h), (2 * ML_HEADS + h, 3 * ML_HEADS + h))
    last_lane = (CHUNK - 1, 0)
    masks = (pj <= pi, pj >= pi)

    def row_vectors(gr, d):
        ir = jnp.sum(jnp.where(sub == gate_idx[d][0], gr, 0.0), axis=0, keepdims=True)
        br = jnp.sum(jnp.where(sub == gate_idx[d][1], gr, 0.0), axis=0, keepdims=True)
        return ir, br

    def lanes(x):
        return jnp.broadcast_to(x, (1, LANES))

    def local_body(t, carry):
        cs = [t * unroll + u for u in range(unroll)]
        weighted = []
        for c in cs:
            kt = kt_ref[c]
            gr = grow_ref[0, :, chunk(c)]
            parts = []
            for d in range(2):
                ir, br = row_vectors(gr, d)
                b_last = br[:, last_lane[d]:last_lane[d] + 1]
                a = b_last - br + ir
                m_loc = jnp.max(a, axis=1, keepdims=True)
                parts.append((kt * jnp.exp(a - m_loc)).astype(BF16))
                sl_ref[c, ST_M + d:ST_M + d + 1, :] = lanes(m_loc)
                sl_ref[c, ST_B + d:ST_B + d + 1, :] = lanes(b_last)
            weighted.append(jnp.concatenate(parts, axis=0))
        for c, kw in zip(cs, weighted):
            v_ones = jnp.concatenate([v_ref[chunk(c), :], ones_tile], axis=1)
            res = _dot(kw, v_ones)
            cl_ref[c] = res[:, 0:ML_DV]
            nl_ref[c] = res[:, ML_DV:ML_DV + LANES]
        return carry

    lax.fori_loop(0, nc // unroll, local_body, 0)

    cst_ref[...] = jnp.zeros_like(cst_ref)
    nst_ref[...] = jnp.zeros_like(nst_ref)
    vst_ref[...] = jnp.full(vst_ref.shape, -jnp.inf, F32)

    def scan_body(t, carry):
        for d, c in ((0, t), (1, nc - 1 - t)):
            rows = slice(d * ML_DK, (d + 1) * ML_DK)
            loc = sl_ref[c]
            c_old = cst_ref[d]
            n_old = nst_ref[d]
            m_old = vst_ref[ST_M + d:ST_M + d + 1, :]
            cp_ref[c, rows, :] = c_old.astype(BF16)
            np_ref[c, rows, :] = n_old.astype(BF16)
            sp_ref[c, ST_M + d:ST_M + d + 1, :] = m_old
            m_loc = loc[ST_M + d:ST_M + d + 1, :]
            b_last = loc[ST_B + d:ST_B + d + 1, :]
            m_new = jnp.maximum(b_last + m_old, m_loc)
            s_old = jnp.exp(b_last + m_old - m_new)[:, 0:1]
            s_new = jnp.exp(m_loc - m_new)[:, 0:1]
            cst_ref[d] = s_old * c_old + s_new * cl_ref[c, rows, :]
            nst_ref[d] = s_old * n_old + s_new * nl_ref[c, rows, :]
            vst_ref[ST_M + d:ST_M + d + 1, :] = m_new
        return carry

    lax.fori_loop(0, nc, scan_body, 0)

    r16 = lax.broadcasted_iota(jnp.int32, (16, 4 * LANES), 0)
    l16 = lax.broadcasted_iota(jnp.int32, (16, 4 * LANES), 1)
    col_block = 2 * (r16 >> 3) + jnp.where((r16 & 7) >= 3, 1, 0)
    col_rhs = jnp.where((l16 >> 7) == col_block, 1.0, 0.0).astype(BF16)
    s_i = lax.broadcasted_iota(jnp.int32, (2 * LANES, 2 * LANES), 0)
    s_j = lax.broadcasted_iota(jnp.int32, (2 * LANES, 2 * LANES), 1)
    sum_rhs = jnp.where((s_i >> 7) == (s_j >> 7), 1.0, 0.0).astype(BF16)

    def out_body(t, carry):
        cs = [t * unroll + u for u in range(unroll)]
        qs = [qc_ref[chunk(c), :] for c in cs]
        stage1 = []
        for q, c in zip(qs, cs):
            qb = q.astype(BF16)
            qk = _dot(qb, kt_ref[c].astype(BF16))
            n_prev = np_ref[c]
            qn = _dot(qb, jnp.concatenate([n_prev[0:ML_DK, :], n_prev[ML_DK:2 * ML_DK, :]], axis=1))
            gr = grow_ref[0, :, chunk(c)]
            prev = sp_ref[c]
            a_arg, b_arg, a_col = [], [], []
            for d in range(2):
                ir, br = row_vectors(gr, d)
                g = br + prev[ST_M + d:ST_M + d + 1, :]
                key_w = ir - br
                m_intra = jnp.sum(jnp.where(sub == N_GATE + gate_idx[d][0], gr, 0.0), axis=0, keepdims=True)
                m_t = jnp.maximum(g, m_intra)
                zero_pad = [None] * (8 - 6)
                a_arg += _split3_rows(br - m_t) + [ones_row] * 3 + zero_pad
                b_rows = _rows_to_matrix([ones_row] * 3 + _split3_rows(key_w), 8)
                zeros = jnp.zeros_like(b_rows)
                b_arg.append(jnp.concatenate([b_rows, zeros] if d == 0 else [zeros, b_rows], axis=1))
                a_col += _split3_rows(jnp.exp(g - m_t)) + _split3_rows(m_t) + zero_pad
            arg = _dotg(_rows_to_matrix(a_arg, 16), jnp.concatenate(b_arg, axis=0), TN)
            cols = _dotg(_rows_to_matrix(a_col, 16), col_rhs, TN)
            stage1.append((qk, qn, arg, cols))
        stage2 = []
        for qk, qn, arg, cols in stage1:
            ss = [qk * jnp.exp(jnp.where(masks[d], arg[:, d * LANES:(d + 1) * LANES], -jnp.inf))
                  for d in range(2)]
            sums = _dot(jnp.concatenate(ss, axis=1).astype(BF16), sum_rhs)
            stage2.append((ss, sums))
        outs = []
        for q, c, (qk, qn, arg, cols), (ss, sums) in zip(qs, cs, stage1, stage2):
            p_sum = None
            q_parts = []
            for d in range(2):
                s_inter = cols[:, 2 * d * LANES:(2 * d + 1) * LANES]
                m_t = cols[:, (2 * d + 1) * LANES:(2 * d + 2) * LANES]
                den = s_inter * qn[:, d * LANES:(d + 1) * LANES] + sums[:, d * LANES:(d + 1) * LANES]
                inv = 1.0 / jnp.maximum(jnp.abs(den), jnp.exp(-m_t))
                p_sum = ss[d] * inv if p_sum is None else p_sum + ss[d] * inv
                q_parts.append((q * (s_inter * inv)).astype(BF16))
            outs.append(_dot(p_sum.astype(BF16), v_ref[chunk(c), :])
                        + _dot(jnp.concatenate(q_parts, axis=1), cp_ref[c]))
        for out, c in zip(outs, cs):
            y = _head_layernorm(out) * ng_ref[...]
            o_ref[chunk(c), :] = (_sigmoid(og_ref[chunk(c), :].astype(F32)) * y).astype(o_ref.dtype)
        return carry

    lax.fori_loop(0, nc // unroll, out_body, 0)


def _mlstm(proj, grow, conv_w, conv_b, norm_g, batch, seq):
    m = proj.shape[0]
    nc = seq // CHUNK
    qb, kb = OFF_MQ // ML_DK, OFF_MK // ML_DK
    vb, ob = OFF_MV // ML_DV, OFF_MO // ML_DV
    kw = conv_w.shape[0]
    return pl.pallas_call(
        _ml_kernel,
        grid=(batch, ML_HEADS),
        in_specs=[
            pl.BlockSpec((seq, ML_DK), lambda b, h: (b, qb + h)),
            pl.BlockSpec((seq, ML_DK), lambda b, h: (b, kb + h)),
            pl.BlockSpec((seq, ML_DV), lambda b, h: (b, vb + h)),
            pl.BlockSpec((seq, ML_DV), lambda b, h: (b, ob + h)),
            pl.BlockSpec((1, 2 * N_GATE, seq), lambda b, h: (b, 0, 0)),
            pl.BlockSpec((kw, ML_DK), lambda b, h: (0, h)),
            pl.BlockSpec((kw, ML_DK), lambda b, h: (0, ML_HEADS + h)),
            pl.BlockSpec((1, ML_DK), lambda b, h: (0, h)),
            pl.BlockSpec((1, ML_DK), lambda b, h: (0, ML_HEADS + h)),
            pl.BlockSpec((1, ML_DV), lambda b, h: (0, h)),
        ],
        out_specs=pl.BlockSpec((seq, ML_DV), lambda b, h: (b, h)),
        out_shape=jax.ShapeDtypeStruct((m, ML_V), BF16),
        scratch_shapes=[
            pltpu.VMEM((3, (ML_CONV_W - 1) * CHUNK, CONV_WIN), BF16),
            pltpu.VMEM((seq, ML_DK), F32),
            pltpu.VMEM((nc, ML_DK, CHUNK), F32),
            pltpu.VMEM((nc, 2 * ML_DK, ML_DV), F32),
            pltpu.VMEM((nc, 2 * ML_DK, LANES), F32),
            pltpu.VMEM((nc, 2 * ML_DK, ML_DV), BF16),
            pltpu.VMEM((nc, 2 * ML_DK, LANES), BF16),
            pltpu.VMEM((nc, 8, LANES), F32),
            pltpu.VMEM((nc, 8, LANES), F32),
            pltpu.VMEM((2, ML_DK, ML_DV), F32),
            pltpu.VMEM((2, ML_DK, LANES), F32),
            pltpu.VMEM((8, LANES), F32),
        ],
        compiler_params=_cparams(("parallel", "arbitrary")),
        name="mlstm",
    )(proj, proj, proj, proj, grow, conv_w, conv_w, conv_b, conv_b, norm_g)


def _merge_kernel(x_ref, yr_ref, yn_ref, ym_ref, mix_ref, gb_ref, wr_ref, wn_ref, wm_ref, wo_ref,
                  o_ref):
    merged = None
    for i, (y_ref, w_ref) in enumerate(((yr_ref, wr_ref), (yn_ref, wn_ref), (ym_ref, wm_ref))):
        cols = slice(i * D_MODEL, (i + 1) * D_MODEL)
        gate = _sigmoid(mix_ref[:, cols].astype(F32) + gb_ref[:, cols])
        term = gate * _dot(y_ref[...], w_ref[...])
        merged = term if merged is None else merged + term
    o_ref[...] = x_ref[...] + _dot(merged.astype(BF16), wo_ref[...])


def _merge(x2, y_ret, y_na, y_ml, proj, gate_b, w_ret_o, w_na_o, w_ml_o, w_out, layer, tm=512):
    m = x2.shape[0]
    row = lambda i: (i, 0)
    const = lambda i: (0, 0)
    wspec = pl.BlockSpec((None, D_MODEL, D_MODEL), lambda i: (layer, 0, 0))
    return pl.pallas_call(
        _merge_kernel,
        grid=(m // tm,),
        in_specs=[
            pl.BlockSpec((tm, D_MODEL), row),
            pl.BlockSpec((tm, RET_V), row),
            pl.BlockSpec((tm, NA_W), row),
            pl.BlockSpec((tm, ML_V), row),
            pl.BlockSpec((tm, N_BRANCH * D_MODEL), lambda i: (i, OFF_MIX // (N_BRANCH * D_MODEL))),
            pl.BlockSpec((1, N_BRANCH * D_MODEL), const),
            wspec, wspec, wspec, wspec,
        ],
        out_specs=pl.BlockSpec((tm, D_MODEL), row),
        out_shape=jax.ShapeDtypeStruct((m, D_MODEL), F32),
        compiler_params=_cparams(("parallel",)),
        name="merge_out",
    )(x2, y_ret, y_na, y_ml, proj, gate_b, w_ret_o, w_na_o, w_ml_o, w_out)


def _rms(x, g):
    return x * lax.rsqrt(jnp.mean(x * x, axis=-1, keepdims=True) + EPS) * g


def _ffn_kernel(x_ref, g_ref, wu_ref, wd_ref, fg_ref, o_ref, *, final_norm):
    x = x_ref[...]
    h = _rms(x, g_ref[...]).astype(BF16)
    acc = x
    for c in range(D_FF // FF_CHUNK):
        cols = slice(c * FF_CHUNK, (c + 1) * FF_CHUNK)
        a = _dot(h, wu_ref[:, cols])
        u = _dot(h, wu_ref[:, D_FF + c * FF_CHUNK:D_FF + (c + 1) * FF_CHUNK])
        act = (a * _sigmoid(a) * u).astype(BF16)
        acc = acc + _dot(act, wd_ref[cols, :])
    o_ref[...] = _rms(acc, fg_ref[...]) if final_norm else acc


def _ffn(x2, g, w_up, w_down, final_g, layer, final_norm, tm=512):
    m = x2.shape[0]
    const = lambda i: (0, 0)
    return pl.pallas_call(
        functools.partial(_ffn_kernel, final_norm=final_norm),
        grid=(m // tm,),
        in_specs=[
            pl.BlockSpec((tm, D_MODEL), lambda i: (i, 0)),
            pl.BlockSpec((1, D_MODEL), const),
            pl.BlockSpec((None, D_MODEL, 2 * D_FF), lambda i: (layer, 0, 0)),
            pl.BlockSpec((None, D_FF, D_MODEL), lambda i: (layer, 0, 0)),
            pl.BlockSpec((1, D_MODEL), const),
        ],
        out_specs=pl.BlockSpec((tm, D_MODEL), lambda i: (i, 0)),
        out_shape=jax.ShapeDtypeStruct((m, D_MODEL), F32),
        compiler_params=_cparams(("parallel",)),
        name="ffn",
    )(x2, g, w_up, w_down, final_g)


def kernel(x, norm1_g, w_in, gate_b, ret_decay_logit, na_rpb, ml_conv_w, ml_conv_b, ml_gate_b, ml_norm_g,
           w_ret_o, w_na_o, w_ml_o, w_out, norm2_g, w_ffn_up, w_ffn_down, final_g):
    batch, seq, _ = x.shape
    depth = w_in.shape[0]
    assert seq % (8 * CHUNK) == 0 and seq // GRID_W >= NA_WIN_ROWS

    half = RET_DK // 2
    inv_freq = ROPE_BASE ** (-jnp.arange(half, dtype=F32) / half)
    ang = jnp.arange(seq, dtype=F32)[:, None] * inv_freq[None, :]
    cos, sin = jnp.cos(ang), jnp.sin(ang)
    cos2 = jnp.concatenate([cos, cos], axis=-1)
    sin2 = jnp.concatenate([-sin, sin], axis=-1)

    w_in_t = jnp.swapaxes(w_in, 1, 2).astype(BF16)
    ml_bias = jnp.pad(ml_gate_b.reshape(depth, 1, N_GATE).astype(F32), ((0, 0), (0, 0), (0, LANES - N_GATE)))
    conv_w = jnp.pad(ml_conv_w.astype(F32), ((0, 0), (0, 8 - ML_CONV_W), (0, 0)))
    lg = jax.nn.log_sigmoid(ret_decay_logit.astype(F32))
    w_ret_o, w_na_o, w_ml_o, w_out = (w.astype(BF16) for w in (w_ret_o, w_na_o, w_ml_o, w_out))
    w_ffn_up, w_ffn_down = w_ffn_up.astype(BF16), w_ffn_down.astype(BF16)

    na_bias = _na_bias_tables(na_rpb)

    x2 = x.reshape(batch * seq, D_MODEL).astype(F32)
    for l in range(depth):
        proj, pre = _in_proj(x2, norm1_g[l][None].astype(F32), w_in_t, l)
        y_ret = _retention(proj, lg[l], cos2, sin2, batch, seq)
        y_na = _neighbourhood(proj, na_bias, l, batch, seq)
        grow = _gates(pre, ml_bias[l], batch, seq)
        y_ml = _mlstm(proj, grow, conv_w[l], ml_conv_b[l][None].astype(F32),
                      ml_norm_g[l][None].astype(F32), batch, seq)
        x2 = _merge(x2, y_ret, y_na, y_ml, proj, gate_b[l][None].astype(F32),
                    w_ret_o, w_na_o, w_ml_o, w_out, l)
        x2 = _ffn(x2, norm2_g[l][None].astype(F32), w_ffn_up, w_ffn_down,
                  final_g[None].astype(F32), l, final_norm=(l == depth - 1))
    return x2.reshape(batch, seq, D_MODEL).astype(x.dtype)
```
